```python
import math
import jax
import jax.numpy as jnp
from jax import lax
import numpy as np

D_MODEL = 4096
BATCH = 2
SEQ = 8192
DEPTH = 2

GRID_W = 64
CTX_LEN = 256
EPS = 1e-6
N_MOD = 6

DN_WIDTH = D_MODEL // 2
DN_HEAD_DIM = 128
DN_HEADS = DN_WIDTH // DN_HEAD_DIM
DN_CONV = 5
DN_CHUNK = 64

SG_WIDTH = D_MODEL // 2
SG_GROUPS = 16
SG_GROUP_DIM = SG_WIDTH // SG_GROUPS
SG_ROWS = 2
SG_CHUNK = SG_ROWS * GRID_W

N_BRANCHES = 2
IN_SIZES = (3 * DN_WIDTH, 4 * DN_HEADS, DN_WIDTH, SG_WIDTH, SG_WIDTH, N_BRANCHES * D_MODEL)
IN_SPLITS = [sum(IN_SIZES[:i + 1]) for i in range(len(IN_SIZES) - 1)]
N_IN = sum(IN_SIZES)
N_SCAN_IN = IN_SIZES[0] + IN_SIZES[1]

N_EXPERTS = 64
EXPERT_FF = 256
TOP_K = 6
N_EXPERT_GROUPS = 8
TOPK_GROUPS = 4
ROUTED_SCALE = 2.5
SHARED_FF = 1024
EXPERT_BLOCK = 8

kernel_name = 'hybrid_gdn_sgmlp_moe_dit'


def rmsnorm(x, w):
    xf = x.astype(jnp.float32)
    y = xf * lax.rsqrt(jnp.mean(xf * xf, axis=-1, keepdims=True) + EPS)
    return (y * w.astype(jnp.float32)).astype(x.dtype)


def layernorm(x, w, b):
    xf = x.astype(jnp.float32)
    mu = jnp.mean(xf, axis=-1, keepdims=True)
    xc = xf - mu
    var = jnp.mean(xc * xc, axis=-1, keepdims=True)
    return (xc * lax.rsqrt(var + EPS) * w.astype(jnp.float32) + b.astype(jnp.float32)).astype(x.dtype)


def modulate(x, shift, scale):
    return x * (1.0 + scale) + shift


def l2norm(x):
    return x * lax.rsqrt(jnp.sum(x * x, axis=-1, keepdims=True) + EPS)


def short_conv(x, w):
    ch = x.shape[-1]
    y = lax.conv_general_dilated(
        x, w[:, None, :].astype(x.dtype), window_strides=(1,),
        padding=[(DN_CONV // 2, DN_CONV // 2)],
        dimension_numbers=('NWC', 'WIO', 'NWC'), feature_group_count=ch)
    return jax.nn.silu(y)


def dn_inputs(p_qkv, p_ba, conv_w, a_log, dt_bias):
    b, t = p_qkv.shape[:2]
    qkv = short_conv(p_qkv, conv_w).astype(jnp.float32)
    q, k, v = [a.reshape(b, t, DN_HEADS, DN_HEAD_DIM) for a in jnp.split(qkv, 3, axis=-1)]
    q = l2norm(q) * (DN_HEAD_DIM ** -0.5)
    k = l2norm(k)
    pba = p_ba.astype(jnp.float32).reshape(b, t, 2, 2, DN_HEADS)
    beta = jax.nn.sigmoid(pba[:, :, 0])
    g = -jnp.exp(a_log.astype(jnp.float32)) * jax.nn.softplus(pba[:, :, 1] + dt_bias.astype(jnp.float32))
    return q, k, v, beta, g


def gated_delta_chunked(q, k, v, beta, g, s0):
    b, t, h, dk = q.shape
    dv = v.shape[-1]
    n = t // DN_CHUNK

    def chunks(a):
        a = a.reshape((b, n, DN_CHUNK, h) + a.shape[3:])
        return jnp.moveaxis(jnp.moveaxis(a, 1, 0), 3, 2)

    qc, kc, vc, bc = chunks(q), chunks(k), chunks(v), chunks(beta)
    gc = jnp.cumsum(chunks(g), axis=-1)
    incl = jnp.tril(jnp.ones((DN_CHUNK, DN_CHUNK), bool))
    strict = jnp.tril(jnp.ones((DN_CHUNK, DN_CHUNK), bool), -1)
    diff = gc[..., :, None] - gc[..., None, :]
    decay = jnp.where(incl, jnp.exp(jnp.where(incl, diff, 0.0)), 0.0)
    kk = jnp.einsum('nbhid,nbhjd->nbhij', kc, kc)
    a_mat = jnp.where(strict, bc[..., :, None] * kk * decay, 0.0) + jnp.eye(DN_CHUNK, dtype=jnp.float32)
    rhs = jnp.concatenate([vc * bc[..., None], kc * (bc * jnp.exp(gc))[..., None]], axis=-1)
    sol = lax.linalg.triangular_solve(a_mat, rhs, left_side=True, lower=True, unit_diagonal=True)
    u, w = sol[..., :dv], sol[..., dv:]
    qk = jnp.einsum('nbhid,nbhjd->nbhij', qc, kc) * decay

    def step(s, xs):
        q_n, k_n, u_n, w_n, qk_n, g_n = xs
        v_new = u_n - jnp.einsum('bhck,bhkv->bhcv', w_n, s)
        o_n = (jnp.einsum('bhck,bhkv->bhcv', q_n * jnp.exp(g_n)[..., None], s)
               + jnp.einsum('bhij,bhjv->bhiv', qk_n, v_new))
        g_last = g_n[..., -1:]
        s = (s * jnp.exp(g_last)[..., None]
             + jnp.einsum('bhck,bhcv->bhkv', k_n * jnp.exp(g_last - g_n)[..., None], v_new))
        return s, o_n

    s_fin, o = lax.scan(step, s0, (qc, kc, u, w, qk, gc))
    o = jnp.transpose(o, (1, 0, 3, 2, 4)).reshape(b, t, h, dv)
    return o, s_fin


def gated_delta_dir(q, k, v, beta, g, s0, reverse):
    if reverse:
        q, k, v, beta, g = [jnp.flip(a, axis=1) for a in (q, k, v, beta, g)]
    o, s = gated_delta_chunked(q, k, v, beta, g, s0)
    return (jnp.flip(o, axis=1) if reverse else o), s


def dn_output(o, z, norm_w):
    b, t = z.shape[:2]
    o = o * lax.rsqrt(jnp.mean(o * o, axis=-1, keepdims=True) + EPS) * norm_w.astype(jnp.float32)
    zg = jax.nn.silu(z.astype(jnp.float32)).reshape(b, t, DN_HEADS, DN_HEAD_DIM)
    return (o * zg).reshape(b, t, DN_WIDTH).astype(z.dtype)


def spatial_gating(p_u, p_v, ln_w, ln_b, sg_w, sg_b, n_chunks):
    b, t, _ = p_u.shape
    u = jax.nn.gelu(p_u)
    vv = layernorm(jax.nn.gelu(p_v), ln_w, ln_b)
    vc = vv.reshape(b, n_chunks, SG_CHUNK, SG_GROUPS, SG_GROUP_DIM)
    mixed = jnp.einsum('gij,bnjgc->bnigc', sg_w, vc) + sg_b.T[None, None, :, :, None]
    return u * mixed.reshape(b, t, SG_WIDTH)


def merge_branches(p_gate, y_dn, y_sg, w_br_dn, w_br_sg, w_out):
    g_dn, g_sg = jnp.split(jax.nn.sigmoid(p_gate), N_BRANCHES, axis=-1)
    m = g_dn * (y_dn @ w_br_dn) + g_sg * (y_sg @ w_br_sg)
    return m @ w_out


def route(t, w_router, bias):
    n_tok = t.shape[0]
    s = jax.nn.sigmoid((t @ w_router).astype(jnp.float32))
    sel = s + bias.astype(jnp.float32)
    grp = sel.reshape(n_tok, N_EXPERT_GROUPS, N_EXPERTS // N_EXPERT_GROUPS)
    gscore = jnp.sum(lax.top_k(grp, 2)[0], axis=-1)
    _, gidx = lax.top_k(gscore, TOPK_GROUPS)
    gmask = jnp.sum(jax.nn.one_hot(gidx, N_EXPERT_GROUPS, dtype=jnp.float32), axis=1) > 0
    emask = jnp.repeat(gmask, N_EXPERTS // N_EXPERT_GROUPS, axis=-1)
    _, eidx = lax.top_k(jnp.where(emask, sel, -jnp.inf), TOP_K)
    wts = jnp.take_along_axis(s, eidx, axis=-1)
    wts = wts / jnp.sum(wts, axis=-1, keepdims=True) * ROUTED_SCALE
    return jnp.sum(jax.nn.one_hot(eidx, N_EXPERTS, dtype=jnp.float32) * wts[..., None], axis=1)


def moe(h, w_router, router_bias, w_g, w_u, w_d, ws_g, ws_u, ws_d):
    shp = h.shape
    t = h.reshape(-1, shp[-1])
    n_tok = t.shape[0]
    gates = route(t, w_router, router_bias)
    shared = (jax.nn.silu(t @ ws_g) * (t @ ws_u)) @ ws_d
    nb = N_EXPERTS // EXPERT_BLOCK
    xs = (w_g.reshape(nb, EXPERT_BLOCK, shp[-1], EXPERT_FF),
          w_u.reshape(nb, EXPERT_BLOCK, shp[-1], EXPERT_FF),
          w_d.reshape(nb, EXPERT_BLOCK, EXPERT_FF, shp[-1]),
          gates.T.reshape(nb, EXPERT_BLOCK, n_tok))

    def body(acc, blk):
        g_w, u_w, d_w, gate = blk
        hid = jax.nn.silu(jnp.einsum('td,edf->tef', t, g_w)) * jnp.einsum('td,edf->tef', t, u_w)
        hid = hid * gate.T[..., None].astype(hid.dtype)
        return acc + jnp.einsum('tef,efd->td', hid, d_w), None

    routed, _ = lax.scan(body, jnp.zeros_like(t), xs)
    return (shared + routed).reshape(shp)


def _normal(key, shape, scale):
    return jax.random.normal(key, shape, jnp.float32) * scale


def setup_inputs(seed: int = 0) -> dict:
    key = jax.random.key(seed)
    ks = list(jax.random.split(key, 32))
    L, D = DEPTH, D_MODEL
    dt = jnp.exp(jax.random.uniform(ks[8], (L, 2, DN_HEADS), jnp.float32,
                                    minval=math.log(1e-3), maxval=math.log(1e-1)))
    return {
        'x': _normal(ks[0], (BATCH, SEQ, D), 1.0),
        'c': _normal(ks[1], (BATCH, D), 1.0),
        'ctx': _normal(ks[2], (BATCH, CTX_LEN, D), 1.0),
        'c_ctx': _normal(ks[3], (D,), 1.0),
        'w_ada': _normal(ks[4], (L, D, N_MOD * D), 0.5 * D ** -0.5),
        'b_ada': _normal(ks[5], (L, N_MOD * D), 0.01),
        'norm1_w': 1.0 + _normal(ks[6], (L, D), 0.1),
        'w_in': _normal(ks[7], (L, D, N_IN), D ** -0.5),
        'dn_conv_w': _normal(ks[9], (L, DN_CONV, 3 * DN_WIDTH), DN_CONV ** -0.5),
        'dn_a_log': jnp.log(jax.random.uniform(ks[10], (L, 2, DN_HEADS), jnp.float32, minval=1.0, maxval=16.0)),
        'dn_dt_bias': dt + jnp.log(-jnp.expm1(-dt)),
        'dn_norm_w': 1.0 + _normal(ks[11], (L, DN_HEAD_DIM), 0.1),
        'sg_ln_w': 1.0 + _normal(ks[12], (L, SG_WIDTH), 0.1),
        'sg_ln_b': _normal(ks[13], (L, SG_WIDTH), 0.01),
        'sg_w': _normal(ks[14], (L, SG_GROUPS, SG_CHUNK, SG_CHUNK), SG_CHUNK ** -0.5),
        'sg_b': 1.0 + _normal(ks[15], (L, SG_GROUPS, SG_CHUNK), 0.1),
        'w_br_dn': _normal(ks[16], (L, DN_WIDTH, D), DN_WIDTH ** -0.5),
        'w_br_sg': _normal(ks[17], (L, SG_WIDTH, D), SG_WIDTH ** -0.5),
        'w_out': _normal(ks[18], (L, D, D), D ** -0.5),
        'norm2_w': 1.0 + _normal(ks[19], (L, D), 0.1),
        'w_router': _normal(ks[20], (L, D, N_EXPERTS), D ** -0.5),
        'router_bias': _normal(ks[21], (L, N_EXPERTS), 0.01),
        'w_exp_gate': _normal(ks[22], (L, N_EXPERTS, D, EXPERT_FF), D ** -0.5),
        'w_exp_up': _normal(ks[23], (L, N_EXPERTS, D, EXPERT_FF), D ** -0.5),
        'w_exp_down': _normal(ks[24], (L, N_EXPERTS, EXPERT_FF, D), EXPERT_FF ** -0.5),
        'w_sh_gate': _normal(ks[25], (L, D, SHARED_FF), D ** -0.5),
        'w_sh_up': _normal(ks[26], (L, D, SHARED_FF), D ** -0.5),
        'w_sh_down': _normal(ks[27], (L, SHARED_FF, D), SHARED_FF ** -0.5),
        'final_norm_w': 1.0 + _normal(ks[28], (D,), 0.1),
    }


def reference(x, c, ctx, c_ctx, w_ada, b_ada, norm1_w, w_in, dn_conv_w, dn_a_log, dn_dt_bias,
              dn_norm_w, sg_ln_w, sg_ln_b, sg_w, sg_b, w_br_dn, w_br_sg, w_out, norm2_w,
              w_router, router_bias, w_exp_gate, w_exp_up, w_exp_down, w_sh_gate, w_sh_up,
              w_sh_down, final_norm_w):
    bsz = x.shape[0]
    rows = x.shape[1] // GRID_W
    n_lat_chunks = rows // SG_ROWS
    n_ctx_chunks = ctx.shape[1] // SG_CHUNK
    c_act = jax.nn.silu(c)[:, None, :]
    cc_act = jax.nn.silu(c_ctx)
    s_zero = jnp.zeros((bsz, DN_HEADS, DN_HEAD_DIM, DN_HEAD_DIM), jnp.float32)
    for l in range(DEPTH):
        last = l == DEPTH - 1
        sh1, sc1, ga1, sh2, sc2, ga2 = jnp.split(c_act @ w_ada[l] + b_ada[l], N_MOD, axis=-1)
        csh1, csc1, cga1, csh2, csc2, cga2 = jnp.split(cc_act @ w_ada[l] + b_ada[l], N_MOD, axis=-1)

        h = modulate(rmsnorm(x, norm1_w[l]), sh1, sc1)
        hc = modulate(rmsnorm(ctx, norm1_w[l]), csh1, csc1)
        p_qkv, p_ba, p_z, p_u, p_v, p_gate = jnp.split(h @ w_in[l], IN_SPLITS, axis=-1)
        pc = hc @ (w_in[l][:, :N_SCAN_IN] if last else w_in[l])
        pc_parts = jnp.split(pc, IN_SPLITS[:1] if last else IN_SPLITS, axis=-1)

        q, k, v, beta, g = dn_inputs(p_qkv, p_ba, dn_conv_w[l], dn_a_log[l], dn_dt_bias[l])
        qc, kc, vc, betac, gcx = dn_inputs(pc_parts[0], pc_parts[1], dn_conv_w[l], dn_a_log[l], dn_dt_bias[l])
        oc_f, s_f = gated_delta_dir(qc, kc, vc, betac[:, :, 0], gcx[:, :, 0], s_zero, False)
        oc_b, s_b = gated_delta_dir(qc, kc, vc, betac[:, :, 1], gcx[:, :, 1], s_zero, True)
        o_f, _ = gated_delta_dir(q, k, v, beta[:, :, 0], g[:, :, 0], s_f, False)
        o_b, _ = gated_delta_dir(q, k, v, beta[:, :, 1], g[:, :, 1], s_b, True)
        y_dn = dn_output(o_f + o_b, p_z, dn_norm_w[l])
        y_sg = spatial_gating(p_u, p_v, sg_ln_w[l], sg_ln_b[l], sg_w[l], sg_b[l], n_lat_chunks)
        x = x + ga1 * merge_branches(p_gate, y_dn, y_sg, w_br_dn[l], w_br_sg[l], w_out[l])

        h2 = modulate(rmsnorm(x, norm2_w[l]), sh2, sc2)
        x = x + ga2 * moe(h2, w_router[l], router_bias[l], w_exp_gate[l], w_exp_up[l], w_exp_down[l],
                          w_sh_gate[l], w_sh_up[l], w_sh_down[l])

        if not last:
            pc_z, pc_u, pc_v, pc_gate = pc_parts[2], pc_parts[3], pc_parts[4], pc_parts[5]
            yc_dn = dn_output(oc_f + oc_b, pc_z, dn_norm_w[l])
            yc_sg = spatial_gating(pc_u, pc_v, sg_ln_w[l], sg_ln_b[l], sg_w[l], sg_b[l], n_ctx_chunks)
            ctx = ctx + cga1 * merge_branches(pc_gate, yc_dn, yc_sg, w_br_dn[l], w_br_sg[l], w_out[l])
            hc2 = modulate(rmsnorm(ctx, norm2_w[l]), csh2, csc2)
            ctx = ctx + cga2 * moe(hc2, w_router[l], router_bias[l], w_exp_gate[l], w_exp_up[l],
                                   w_exp_down[l], w_sh_gate[l], w_sh_up[l], w_sh_down[l])
    return rmsnorm(x, final_norm_w)
```

```python
import functools

import jax
import jax.numpy as jnp
from jax import lax
from jax.experimental import pallas as pl
from jax.experimental.pallas import tpu as pltpu

F32 = jnp.float32
BF16 = jnp.bfloat16

EPS = 1e-6
N_MOD = 6
DN_HEAD_DIM = 128
DN_CONV = 5
DN_CHUNK = 64
SG_CHUNK = 128
TOP_K = 6
N_EXPERT_GROUPS = 8
TOPK_GROUPS = 4
ROUTED_SCALE = 2.5

LANE = 128
BF16_SUBLANES = 16
MIB = 1024 * 1024


def _cparams(sem, vmem_mib):
    return pltpu.CompilerParams(dimension_semantics=sem, vmem_limit_bytes=vmem_mib * MIB)


def _col_tile(n, pref):
    t = min(pref, n)
    while n % t or t % LANE:
        t -= LANE
    return t


def _mod_row(tile_start, n_ctx_rows, seq):
    return jnp.where(tile_start < n_ctx_rows, 0, 1 + (tile_start - n_ctx_rows) // seq)


def _mod_spec(which, tm, n_ctx_rows, seq, d, ncol=None):
    if ncol is None:
        return pl.BlockSpec((None, None, 1, d),
                            lambda i, *_: (_mod_row(i * tm, n_ctx_rows, seq), which, 0, 0))
    return pl.BlockSpec((None, None, 1, ncol),
                        lambda i, j: (_mod_row(i * tm, n_ctx_rows, seq), which, 0, j))


def _ada_kernel(c_ref, w_ref, b_ref, o_ref):
    c = c_ref[...]
    a = (c * jax.nn.sigmoid(c)).astype(BF16)
    o_ref[...] = jnp.dot(a, w_ref[...].astype(BF16), preferred_element_type=F32) + b_ref[...]


def _ada(cvec, w_ada, b_ada):
    nl, d, n = w_ada.shape
    tn = min(512, n)
    return pl.pallas_call(
        _ada_kernel,
        grid=(nl, n // tn),
        in_specs=[pl.BlockSpec((8, d), lambda l, j: (0, 0)),
                  pl.BlockSpec((None, d, tn), lambda l, j: (l, 0, j)),
                  pl.BlockSpec((None, 1, tn), lambda l, j: (l, 0, j))],
        out_specs=pl.BlockSpec((None, 8, tn), lambda l, j: (l, 0, j)),
        out_shape=jax.ShapeDtypeStruct((nl, 8, n), F32),
        compiler_params=_cparams(("parallel", "parallel"), 40),
        name="ada",
    )(cvec, w_ada, b_ada.reshape(nl, 1, n))


def _in_proj_kernel(x_ref, sh_ref, sc_ref, nw_ref, w_ref, wba_ref, p_ref, pba_ref, h_ref):
    @pl.when(pl.program_id(1) == 0)
    def _():
        x = x_ref[...]
        ms = jnp.mean(x * x, axis=-1, keepdims=True)
        y = x * lax.rsqrt(ms + EPS) * nw_ref[...]
        hb = (y * (1.0 + sc_ref[...]) + sh_ref[...]).astype(BF16)
        h_ref[...] = hb
        pba_ref[...] = jnp.dot(hb, wba_ref[...], preferred_element_type=F32)

    p_ref[...] = jnp.dot(h_ref[...], w_ref[...], preferred_element_type=F32).astype(p_ref.dtype)


def _in_proj(x, mod, nw, w_main, w_ba, tm, n_ctx_rows, seq):
    r, d = x.shape
    n = w_main.shape[1]
    tn = _col_tile(n, 1024)
    return pl.pallas_call(
        _in_proj_kernel,
        grid=(r // tm, n // tn),
        in_specs=[pl.BlockSpec((tm, d), lambda i, j: (i, 0)),
                  _mod_spec(0, tm, n_ctx_rows, seq, d),
                  _mod_spec(1, tm, n_ctx_rows, seq, d),
                  pl.BlockSpec((1, d), lambda i, j: (0, 0)),
                  pl.BlockSpec((d, tn), lambda i, j: (0, j)),
                  pl.BlockSpec((d, LANE), lambda i, j: (0, 0))],
        out_specs=[pl.BlockSpec((tm, tn), lambda i, j: (i, j)),
                   pl.BlockSpec((tm, LANE), lambda i, j: (i, 0))],
        out_shape=[jax.ShapeDtypeStruct((r, n), BF16), jax.ShapeDtypeStruct((r, LANE), F32)],
        scratch_shapes=[pltpu.VMEM((tm, d), BF16)],
        compiler_params=_cparams(("parallel", "arbitrary"), 48),
        name="in_proj",
    )(x, mod, mod, nw, w_main, w_ba)


def _dn_prep_kernel(cur_ref, prev_ref, next_ref, cw_ref, pba_ref, alog_ref, dtb_ref,
                    qkv_ref, bg_ref, *, tc, seg_starts, seg_ends, dn_width, n_heads):
    i = pl.program_id(0)
    j = pl.program_id(1)
    cb = cur_ref.shape[1]
    start = i * tc
    is_start = functools.reduce(jnp.logical_or, [start == s for s in seg_starts])
    is_end = functools.reduce(jnp.logical_or, [start + tc == s for s in seg_ends])
    x = cur_ref[...].astype(F32)
    hp = prev_ref[...].astype(F32)
    hn = next_ref[...].astype(F32)
    nh = hp.shape[0]
    keep_p = jnp.where(is_start, 0.0, 1.0)
    keep_n = jnp.where(is_end, 0.0, 1.0)
    pm2 = hp[nh - 2:nh - 1] * keep_p
    pm1 = hp[nh - 1:nh] * keep_p
    np1 = hn[0:1] * keep_n
    np2 = hn[1:2] * keep_n
    row = lax.broadcasted_iota(jnp.int32, (tc, cb), 0)
    xm1 = jnp.where(row == 0, pm1, pltpu.roll(x, 1, axis=0))
    xm2 = jnp.where(row == 0, pm2, jnp.where(row == 1, pm1, pltpu.roll(x, 2, axis=0)))
    xp1 = jnp.where(row == tc - 1, np1, pltpu.roll(x, tc - 1, axis=0))
    xp2 = jnp.where(row == tc - 1, np2, jnp.where(row == tc - 2, np1, pltpu.roll(x, tc - 2, axis=0)))
    cw = cw_ref[...]
    y = cw[0:1] * xm2 + cw[1:2] * xm1 + cw[2:3] * x + cw[3:4] * xp1 + cw[4:5] * xp2
    y = y * jax.nn.sigmoid(y)
    kind = (j * cb) // dn_width
    qscale = jnp.where(kind == 0, DN_HEAD_DIM ** -0.5, 1.0)
    for hh in range(cb // DN_HEAD_DIM):
        seg = y[:, hh * DN_HEAD_DIM:(hh + 1) * DN_HEAD_DIM]
        ss = jnp.sum(seg * seg, axis=-1, keepdims=True)
        fac = jnp.where(kind == 2, 1.0, lax.rsqrt(ss + EPS) * qscale)
        qkv_ref[:, hh * DN_HEAD_DIM:(hh + 1) * DN_HEAD_DIM] = (seg * fac).astype(qkv_ref.dtype)

    @pl.when(j == 0)
    def _():
        p = pba_ref[...]
        lane = lax.broadcasted_iota(jnp.int32, p.shape, 1)
        beta = jax.nn.sigmoid(p)
        z = p + dtb_ref[...]
        sp = jnp.maximum(z, 0.0) + jnp.log(1.0 + jnp.exp(-jnp.abs(z)))
        g = -jnp.exp(alog_ref[...]) * sp
        bg_ref[...] = jnp.where(lane < 2 * n_heads, beta, g)


def _dn_prep(p_main, pba, conv_w, alog_row, dtb_row, tc, segs, dn_width, n_heads):
    r = p_main.shape[0]
    cb = min(512, dn_width)
    hb = BF16_SUBLANES
    nblk16 = r // hb
    seg_starts = tuple(s for s, _ in segs)
    seg_ends = tuple(s + n for s, n in segs)
    kern = functools.partial(_dn_prep_kernel, tc=tc, seg_starts=seg_starts, seg_ends=seg_ends,
                             dn_width=dn_width, n_heads=n_heads)
    return pl.pallas_call(
        kern,
        grid=(r // tc, 3 * dn_width // cb),
        in_specs=[pl.BlockSpec((tc, cb), lambda i, j: (i, j)),
                  pl.BlockSpec((hb, cb), lambda i, j: (jnp.maximum(i * (tc // hb) - 1, 0), j)),
                  pl.BlockSpec((hb, cb), lambda i, j: (jnp.minimum((i + 1) * (tc // hb), nblk16 - 1), j)),
                  pl.BlockSpec((DN_CONV, cb), lambda i, j: (0, j)),
                  pl.BlockSpec((tc, LANE), lambda i, j: (i, 0)),
                  pl.BlockSpec((1, LANE), lambda i, j: (0, 0)),
                  pl.BlockSpec((1, LANE), lambda i, j: (0, 0))],
        out_specs=[pl.BlockSpec((tc, cb), lambda i, j: (i, j)),
                   pl.BlockSpec((tc, LANE), lambda i, j: (i, 0))],
        out_shape=[jax.ShapeDtypeStruct((r, 3 * dn_width), BF16),
                   jax.ShapeDtypeStruct((r, LANE), F32)],
        compiler_params=_cparams(("parallel", "arbitrary"), 32),
        name="dn_prep",
    )(p_main, p_main, p_main, conv_w, pba, alog_row, dtb_row)


def _split3(a):
    a1 = a.astype(BF16)
    r1 = a - a1.astype(F32)
    a2 = r1.astype(BF16)
    a3 = (r1 - a2.astype(F32)).astype(BF16)
    return a1, a2, a3


def _dot(a, b):
    return jnp.dot(a, b, preferred_element_type=F32)


def _scan_kernel(q_ref, k_ref, v_ref, kt_ref, gbc_ref, gr_ref, o_ref, s_ref, *, hg, ng):
    c = pl.program_id(0)
    step = pl.program_id(1)
    sgn = 1 - 2 * ((c // ng) % 2)
    cs = DN_CHUNK
    hd = DN_HEAD_DIM

    @pl.when(step == 0)
    def _():
        s_ref[...] = jnp.zeros_like(s_ref)

    row = lax.broadcasted_iota(jnp.int32, (cs, cs), 0)
    col = lax.broadcasted_iota(jnp.int32, (cs, cs), 1)
    rel = (row - col) * sgn
    incl = rel >= 0
    strict = rel > 0
    m_incl = jnp.where(incl, 1.0, 0.0).astype(BF16)
    m_incl_t = jnp.where(rel <= 0, 1.0, 0.0).astype(BF16)
    eye = jnp.where(rel == 0, 1.0, 0.0)

    gb = gbc_ref[...]
    gr = gr_ref[...]
    gc_c = sum(_dot(m_incl, part) for part in _split3(gb))
    gc_r = sum(_dot(part, m_incl_t) for part in _split3(gr))
    gtot = jnp.sum(gb, axis=0, keepdims=True)

    for h in range(hg):
        hs = slice(h * hd, (h + 1) * hd)
        gcc = gc_c[:, hg + h:hg + h + 1]
        gcr = gc_r[h:h + 1, :]
        bc = gb[:, h:h + 1]
        gt = gtot[:, hg + h:hg + h + 1]
        q = q_ref[:, hs]
        k = k_ref[:, hs]
        v = v_ref[:, hs]
        kt = kt_ref[hs, :]
        decay = jnp.where(incl, jnp.exp(jnp.where(incl, gcc - gcr, 0.0)), 0.0)
        lmat = jnp.where(strict, bc * _dot(k, kt) * decay, 0.0)
        egc = jnp.exp(gcc)
        rhs = jnp.concatenate([v.astype(F32) * bc, k.astype(F32) * (bc * egc)], axis=1).astype(BF16)
        lp = lmat.astype(BF16)
        t = eye - lmat
        n_sq = 1
        while 2 * n_sq < cs:
            lsq = _dot(lp, lp)
            lp = lsq.astype(BF16)
            t = t + _dot(t.astype(BF16), lp)
            n_sq *= 2
        uw = _dot(t.astype(BF16), rhs)
        u = uw[:, :hd]
        w = uw[:, hd:]
        qk = _dot(q, kt) * decay
        qg = q.astype(F32) * egc
        s_old = s_ref[h]
        ws = _dot(jnp.concatenate([w, qg], axis=0).astype(BF16), s_old.astype(BF16))
        v_new = (u - ws[:cs]).astype(BF16)
        o_ref[:, hs] = ws[cs:] + _dot(qk.astype(BF16), v_new)
        kdt = (kt.astype(F32) * jnp.exp(gt - gcr)).astype(BF16)
        s_ref[h] = s_old * jnp.exp(gt) + _dot(kdt, v_new)


def _dn_scan(qkv, kt, gbc, grow, n_batch, ctx_len, seq, n_heads, hg):
    r = qkv.shape[0]
    ng = n_heads // hg
    ncx = ctx_len // DN_CHUNK
    nlt = seq // DN_CHUNK
    wblk = hg * DN_HEAD_DIM

    def chunk(c, s):
        b = c // (2 * ng)
        fwd = ((c // ng) % 2) == 0
        pos_ctx = jnp.where(fwd, s, ncx - 1 - s)
        pos_lat = jnp.where(fwd, s - ncx, nlt - 1 - (s - ncx))
        return jnp.where(s < ncx, b * ncx + pos_ctx, n_batch * ncx + b * nlt + pos_lat)

    kern = functools.partial(_scan_kernel, hg=hg, ng=ng)
    return pl.pallas_call(
        kern,
        grid=(n_batch * 2 * ng, ncx + nlt),
        in_specs=[pl.BlockSpec((DN_CHUNK, wblk), lambda c, s: (chunk(c, s), c % ng)),
                  pl.BlockSpec((DN_CHUNK, wblk), lambda c, s: (chunk(c, s), ng + c % ng)),
                  pl.BlockSpec((DN_CHUNK, wblk), lambda c, s: (chunk(c, s), 2 * ng + c % ng)),
                  pl.BlockSpec((None, wblk, DN_CHUNK), lambda c, s: (chunk(c, s), c % ng, 0)),
                  pl.BlockSpec((None, None, DN_CHUNK, LANE),
                               lambda c, s: ((c // ng) % 2, c % ng, chunk(c, s), 0)),
                  pl.BlockSpec((None, None, None, grow.shape[3], DN_CHUNK),
                               lambda c, s: ((c // ng) % 2, c % ng, chunk(c, s), 0, 0))],
        out_specs=pl.BlockSpec((None, DN_CHUNK, wblk), lambda c, s: ((c // ng) % 2, chunk(c, s), c % ng)),
        out_shape=jax.ShapeDtypeStruct((2, r, n_heads * DN_HEAD_DIM), F32),
        scratch_shapes=[pltpu.VMEM((hg, DN_HEAD_DIM, DN_HEAD_DIM), F32)],
        compiler_params=_cparams(("parallel", "arbitrary"), 32),
        name="dn_scan",
    )(qkv, qkv, qkv, kt, gbc, grow)


def _sg_kernel(pu_ref, pv_ref, lnw_ref, lnb_ref, sgw_ref, sgbt_ref, y_ref, *, n_groups):
    tr = pu_ref.shape[0]
    gd = pu_ref.shape[1] // n_groups
    v = jax.nn.gelu(pv_ref[...].astype(F32))
    mu = jnp.mean(v, axis=-1, keepdims=True)
    xc = v - mu
    var = jnp.mean(xc * xc, axis=-1, keepdims=True)
    vv = (xc * lax.rsqrt(var + EPS) * lnw_ref[...] + lnb_ref[...]).astype(BF16)
    sgbt = sgbt_ref[...]
    for ch in range(tr // SG_CHUNK):
        rs = slice(ch * SG_CHUNK, (ch + 1) * SG_CHUNK)
        for g in range(n_groups):
            gs = slice(g * gd, (g + 1) * gd)
            mixed = _dot(sgw_ref[g], vv[rs, gs]) + sgbt[:, g:g + 1]
            u = jax.nn.gelu(pu_ref[rs, gs].astype(F32))
            y_ref[rs, gs] = (u * mixed).astype(y_ref.dtype)


def _sg(p_main, lnw, lnb, sgw, sgbt, tr, u_blk, v_blk):
    r = p_main.shape[0]
    n_groups = sgw.shape[0]
    width = lnw.shape[1]
    kern = functools.partial(_sg_kernel, n_groups=n_groups)
    return pl.pallas_call(
        kern,
        grid=(r // tr,),
        in_specs=[pl.BlockSpec((tr, width), lambda i: (i, u_blk)),
                  pl.BlockSpec((tr, width), lambda i: (i, v_blk)),
                  pl.BlockSpec((1, width), lambda i: (0, 0)),
                  pl.BlockSpec((1, width), lambda i: (0, 0)),
                  pl.BlockSpec(sgw.shape, lambda i: (0, 0, 0)),
                  pl.BlockSpec(sgbt.shape, lambda i: (0, 0))],
        out_specs=pl.BlockSpec((tr, width), lambda i: (i, 0)),
        out_shape=jax.ShapeDtypeStruct((r, width), BF16),
        compiler_params=_cparams(("parallel",), 32),
        name="spatial_gating",
    )(p_main, p_main, lnw, lnb, sgw, sgbt)


def _merge_kernel(of_ref, ob_ref, z_ref, dnw_ref, ysg_ref, wdn_ref, wsg_ref, gdn_ref, gsg_ref,
                  m_ref, ydn_ref):
    @pl.when(pl.program_id(1) == 0)
    def _():
        nw = dnw_ref[...]
        for h in range(of_ref.shape[1] // DN_HEAD_DIM):
            hs = slice(h * DN_HEAD_DIM, (h + 1) * DN_HEAD_DIM)
            o = of_ref[:, hs] + ob_ref[:, hs]
            ms = jnp.mean(o * o, axis=-1, keepdims=True)
            z = z_ref[:, hs].astype(F32)
            ydn_ref[:, hs] = (o * lax.rsqrt(ms + EPS) * nw * (z * jax.nn.sigmoid(z))).astype(BF16)

    a = _dot(ydn_ref[...], wdn_ref[...])
    b = _dot(ysg_ref[...], wsg_ref[...])
    m = (jax.nn.sigmoid(gdn_ref[...].astype(F32)) * a + jax.nn.sigmoid(gsg_ref[...].astype(F32)) * b)
    m_ref[...] = m.astype(m_ref.dtype)


def _merge(o2, p_main, dnw, ysg, wdn, wsg, tm, z_blk, gate_col0):
    r, width = ysg.shape
    d = wdn.shape[1]
    tn = min(512, d)
    g0 = gate_col0 // tn
    return pl.pallas_call(
        _merge_kernel,
        grid=(r // tm, d // tn),
        in_specs=[pl.BlockSpec((None, tm, width), lambda i, j: (0, i, 0)),
                  pl.BlockSpec((None, tm, width), lambda i, j: (1, i, 0)),
                  pl.BlockSpec((tm, width), lambda i, j: (i, z_blk)),
                  pl.BlockSpec((1, DN_HEAD_DIM), lambda i, j: (0, 0)),
                  pl.BlockSpec((tm, width), lambda i, j: (i, 0)),
                  pl.BlockSpec((width, tn), lambda i, j: (0, j)),
                  pl.BlockSpec((width, tn), lambda i, j: (0, j)),
                  pl.BlockSpec((tm, tn), lambda i, j: (i, g0 + j)),
                  pl.BlockSpec((tm, tn), lambda i, j: (i, g0 + d // tn + j))],
        out_specs=pl.BlockSpec((tm, tn), lambda i, j: (i, j)),
        out_shape=jax.ShapeDtypeStruct((r, d), BF16),
        scratch_shapes=[pltpu.VMEM((tm, width), BF16)],
        compiler_params=_cparams(("parallel", "arbitrary"), 48),
        name="merge",
    )(o2, o2, p_main, dnw, ysg, wdn, wsg, p_main, p_main)


def _out_proj_kernel(m_ref, w_ref, x_ref, ga_ref, o_ref):
    o_ref[...] = x_ref[...] + ga_ref[...] * _dot(m_ref[...], w_ref[...])


def _out_proj(m, w, x, mod, tm, n_ctx_rows, seq):
    r, d = x.shape
    tn = min(1024, d)
    return pl.pallas_call(
        _out_proj_kernel,
        grid=(r // tm, d // tn),
        in_specs=[pl.BlockSpec((tm, d), lambda i, j: (i, 0)),
                  pl.BlockSpec((d, tn), lambda i, j: (0, j)),
                  pl.BlockSpec((tm, tn), lambda i, j: (i, j)),
                  _mod_spec(2, tm, n_ctx_rows, seq, d, ncol=tn)],
        out_specs=pl.BlockSpec((tm, tn), lambda i, j: (i, j)),
        out_shape=jax.ShapeDtypeStruct((r, d), F32),
        input_output_aliases={2: 0},
        compiler_params=_cparams(("parallel", "arbitrary"), 48),
        name="out_proj",
    )(m, w, x, mod)


def _nt_dot(a, b):
    return lax.dot_general(a, b, (((1,), (1,)), ((), ())), preferred_element_type=F32)


def _route_kernel(x_ref, sh_ref, sc_ref, nw_ref, wrt_ref, bias_ref, h_ref, g_ref):
    x = x_ref[...]
    ms = jnp.mean(x * x, axis=-1, keepdims=True)
    h = x * lax.rsqrt(ms + EPS) * nw_ref[...] * (1.0 + sc_ref[...]) + sh_ref[...]
    hb = h.astype(BF16)
    h_ref[...] = hb
    h_lo = (h - hb.astype(F32)).astype(BF16)
    w = wrt_ref[...]
    wb = w.astype(BF16)
    w_lo = (w - wb.astype(F32)).astype(BF16)
    logits = _nt_dot(wb, hb) + _nt_dot(wb, h_lo) + _nt_dot(w_lo, hb)
    s = jax.nn.sigmoid(logits)
    sel = s + bias_ref[...]
    ne, tm = sel.shape
    gsz = ne // N_EXPERT_GROUPS
    sub = lax.broadcasted_iota(jnp.int32, (gsz, tm), 0)
    gs_rows = []
    for g in range(N_EXPERT_GROUPS):
        blk = sel[g * gsz:(g + 1) * gsz, :]
        m1 = jnp.max(blk, axis=0, keepdims=True)
        i1 = jnp.min(jnp.where(blk == m1, sub, gsz), axis=0, keepdims=True)
        m2 = jnp.max(jnp.where(sub == i1, -jnp.inf, blk), axis=0, keepdims=True)
        gs_rows.append(m1 + m2)
    masked_blocks = []
    for g in range(N_EXPERT_GROUPS):
        rank = jnp.zeros((1, tm), F32)
        for g2 in range(N_EXPERT_GROUPS):
            if g2 == g:
                continue
            ahead = (gs_rows[g2] > gs_rows[g]) if g2 > g else (gs_rows[g2] >= gs_rows[g])
            rank = rank + jnp.where(ahead, 1.0, 0.0)
        keep = rank < TOPK_GROUPS
        masked_blocks.append(jnp.where(keep, sel[g * gsz:(g + 1) * gsz, :], -jnp.inf))
    masked = jnp.concatenate(masked_blocks, axis=0)
    eidx = lax.broadcasted_iota(jnp.int32, (ne, tm), 0)
    rank = jnp.zeros((ne, tm), F32)
    for e2 in range(ne):
        r2 = masked[e2:e2 + 1, :]
        tie = jnp.where(eidx > e2, 1.0, 0.0)
        rank = rank + jnp.where(r2 > masked, 1.0, jnp.where(r2 == masked, tie, 0.0))
    wts = jnp.where(rank < TOP_K, s, 0.0)
    g_ref[...] = wts / jnp.sum(wts, axis=0, keepdims=True) * ROUTED_SCALE


def _route(x, mod, nw, wrt, bias, tm, n_ctx_rows, seq):
    r, d = x.shape
    ne = wrt.shape[0]
    return pl.pallas_call(
        _route_kernel,
        grid=(r // tm,),
        in_specs=[pl.BlockSpec((tm, d), lambda i: (i, 0)),
                  _mod_spec(3, tm, n_ctx_rows, seq, d),
                  _mod_spec(4, tm, n_ctx_rows, seq, d),
                  pl.BlockSpec((1, d), lambda i: (0, 0)),
                  pl.BlockSpec((ne, d), lambda i: (0, 0)),
                  pl.BlockSpec((ne, 1), lambda i: (0, 0))],
        out_specs=[pl.BlockSpec((tm, d), lambda i: (i, 0)),
                   pl.BlockSpec((ne, tm), lambda i: (0, i))],
        out_shape=[jax.ShapeDtypeStruct((r, d), BF16), jax.ShapeDtypeStruct((ne, r), F32)],
        compiler_params=_cparams(("parallel",), 40),
        name="route",
    )(x, mod, mod, nw, wrt, bias)


def _ffn_kernel(h_ref, wg_ref, wu_ref, wd_ref, gate_ref, y_ref):
    @pl.when(pl.program_id(1) == 0)
    def _():
        y_ref[...] = jnp.zeros_like(y_ref)

    h = h_ref[...]
    a = _dot(h, wg_ref[...])
    u = _dot(h, wu_ref[...])
    hid = (a * jax.nn.sigmoid(a) * u * gate_ref[...]).astype(BF16)
    y_ref[...] += _dot(hid, wd_ref[...])


def _ffn(h, wg, wu, wd, gate, tm):
    r, d = h.shape
    ne, _, ff = wg.shape
    return pl.pallas_call(
        _ffn_kernel,
        grid=(r // tm, ne),
        in_specs=[pl.BlockSpec((tm, d), lambda i, e: (i, 0)),
                  pl.BlockSpec((None, d, ff), lambda i, e: (e, 0, 0)),
                  pl.BlockSpec((None, d, ff), lambda i, e: (e, 0, 0)),
                  pl.BlockSpec((None, ff, d), lambda i, e: (e, 0, 0)),
                  pl.BlockSpec((None, tm, 1), lambda i, e: (e, i, 0))],
        out_specs=pl.BlockSpec((tm, d), lambda i, e: (i, 0)),
        out_shape=jax.ShapeDtypeStruct((r, d), F32),
        compiler_params=_cparams(("parallel", "arbitrary"), 48),
        name="expert_ffn",
    )(h, wg, wu, wd, gate)


def _resid_kernel(x_ref, ys_ref, yr_ref, ga_ref, o_ref):
    o_ref[...] = x_ref[...] + ga_ref[...] * (ys_ref[...] + yr_ref[...])


def _resid_norm_kernel(x_ref, ys_ref, yr_ref, ga_ref, fw_ref, o_ref):
    x = x_ref[...] + ga_ref[...] * (ys_ref[...] + yr_ref[...])
    ms = jnp.mean(x * x, axis=-1, keepdims=True)
    o_ref[...] = x * lax.rsqrt(ms + EPS) * fw_ref[...]


def _resid(x, ys, yr, mod, tm, n_ctx_rows, seq, final_w=None):
    r, d = x.shape
    row_spec = pl.BlockSpec((tm, d), lambda i: (i, 0))
    specs = [row_spec, row_spec, row_spec, _mod_spec(5, tm, n_ctx_rows, seq, d)]
    args = [x, ys, yr, mod]
    if final_w is None:
        kern, aliases = _resid_kernel, {0: 0}
    else:
        kern, aliases = _resid_norm_kernel, {}
        specs.append(pl.BlockSpec((1, d), lambda i: (0, 0)))
        args.append(final_w)
    return pl.pallas_call(
        kern,
        grid=(r // tm,),
        in_specs=specs,
        out_specs=row_spec,
        out_shape=jax.ShapeDtypeStruct((r, d), F32),
        input_output_aliases=aliases,
        compiler_params=_cparams(("parallel",), 48),
        name="moe_resid",
    )(*args)


def kernel(x, c, ctx, c_ctx, w_ada, b_ada, norm1_w, w_in, dn_conv_w, dn_a_log, dn_dt_bias, dn_norm_w, sg_ln_w, sg_ln_b, sg_w, sg_b, w_br_dn, w_br_sg, w_out, norm2_w, w_router, router_bias, w_exp_gate, w_exp_up, w_exp_down, w_sh_gate, w_sh_up, w_sh_down, final_norm_w):
    nb, seq, d = x.shape
    ctx_len = ctx.shape[1]
    depth = w_ada.shape[0]
    n_heads = dn_a_log.shape[-1]
    dn_width = dn_conv_w.shape[-1] // 3
    sg_width = sg_ln_w.shape[-1]
    ne, _, ff = w_exp_gate.shape[1:]
    sh_ff = w_sh_gate.shape[-1]
    n_ctx_rows = nb * ctx_len
    r = n_ctx_rows + nb * seq
    assert nb + 1 <= 8 and dn_width == n_heads * DN_HEAD_DIM and sh_ff % ff == 0
    tm = min(512, n_ctx_rows)
    tc = min(256, ctx_len)
    assert n_ctx_rows % tm == 0 and seq % tm == 0 and ctx_len % tc == 0 and seq % tc == 0
    hg = min(8, n_heads)
    ng = n_heads // hg
    segs = [(b * ctx_len, ctx_len) for b in range(nb)] + [(n_ctx_rows + b * seq, seq) for b in range(nb)]

    n_ba = 4 * n_heads
    c_qkv = 3 * dn_width
    z_col = c_qkv
    u_col = z_col + dn_width
    v_col = u_col + sg_width
    gate_col = v_col + sg_width
    assert dn_width == sg_width and z_col % dn_width == 0 and gate_col % 512 == 0

    cvec = jnp.concatenate([c_ctx[None, :], c, jnp.zeros((8 - 1 - nb, d), F32)], axis=0)
    mod_all = _ada(cvec, w_ada, b_ada)
    xa = jnp.concatenate([ctx.reshape(n_ctx_rows, d), x.reshape(nb * seq, d)], axis=0)

    out = None
    for l in range(depth):
        last = l == depth - 1
        mod = mod_all[l, :nb + 1].reshape(nb + 1, N_MOD, 1, d)
        w_main = jnp.concatenate([w_in[l][:, :c_qkv], w_in[l][:, c_qkv + n_ba:]], axis=1).astype(BF16)
        w_ba = jnp.pad(w_in[l][:, c_qkv:c_qkv + n_ba], ((0, 0), (0, LANE - n_ba))).astype(BF16)

        p_main, pba = _in_proj(xa, mod, norm1_w[l][None, :], w_main, w_ba, tm, n_ctx_rows, seq)
        alog_row = jnp.pad(dn_a_log[l].reshape(1, -1), ((0, 0), (2 * n_heads, LANE - 4 * n_heads)))
        dtb_row = jnp.pad(dn_dt_bias[l].reshape(1, -1), ((0, 0), (2 * n_heads, LANE - 4 * n_heads)))
        qkv, bg = _dn_prep(p_main, pba, dn_conv_w[l], alog_row, dtb_row, tc, segs, dn_width, n_heads)
        beta = bg[:, :2 * n_heads].reshape(r, 2, ng, hg)
        glog = bg[:, 2 * n_heads:4 * n_heads].reshape(r, 2, ng, hg)
        gbc = jnp.concatenate([beta, glog], axis=-1).transpose(1, 2, 0, 3)
        gbc = jnp.pad(gbc, ((0, 0), (0, 0), (0, 0), (0, LANE - 2 * hg)))
        grow = glog.reshape(r // DN_CHUNK, DN_CHUNK, 2, ng, hg).transpose(2, 3, 0, 4, 1)
        grow = jnp.pad(grow, ((0, 0), (0, 0), (0, 0), (0, max(0, BF16_SUBLANES - hg)), (0, 0)))
        kt = qkv[:, dn_width:2 * dn_width].reshape(r // DN_CHUNK, DN_CHUNK, dn_width).transpose(0, 2, 1)
        o2 = _dn_scan(qkv, kt, gbc, grow, nb, ctx_len, seq, n_heads, hg)
        ysg = _sg(p_main, sg_ln_w[l][None, :], sg_ln_b[l][None, :], sg_w[l].astype(BF16), sg_b[l].T,
                  min(256, ctx_len), u_col // sg_width, v_col // sg_width)
        m = _merge(o2, p_main, dn_norm_w[l][None, :], ysg, w_br_dn[l].astype(BF16),
                   w_br_sg[l].astype(BF16), tm, z_col // dn_width, gate_col)
        xa = _out_proj(m, w_out[l].astype(BF16), xa, mod, tm, n_ctx_rows, seq)

        h2, gates_t = _route(xa, mod, norm2_w[l][None, :], w_router[l].T, router_bias[l][:, None],
                             min(256, tm), n_ctx_rows, seq)
        y_routed = _ffn(h2, w_exp_gate[l].astype(BF16), w_exp_up[l].astype(BF16),
                        w_exp_down[l].astype(BF16), gates_t.reshape(ne, r, 1), tm)
        nsh = sh_ff // ff
        y_shared = _ffn(h2,
                        w_sh_gate[l].reshape(d, nsh, ff).transpose(1, 0, 2).astype(BF16),
                        w_sh_up[l].reshape(d, nsh, ff).transpose(1, 0, 2).astype(BF16),
                        w_sh_down[l].reshape(nsh, ff, d).astype(BF16),
                        jnp.ones((nsh, r, 1), F32), tm)
        tr = min(128, tm)
        if last:
            out = _resid(xa, y_shared, y_routed, mod, tr, n_ctx_rows, seq, final_norm_w[None, :])
        else:
            xa = _resid(xa, y_shared, y_routed, mod, tr, n_ctx_rows, seq)
    return out[n_ctx_rows:].reshape(nb, seq, d)
```

```python
import functools

import jax
import jax.numpy as jnp
from jax import lax
from jax.experimental import pallas as pl
from jax.experimental.pallas import tpu as pltpu

F32 = jnp.float32
BF16 = jnp.bfloat16

EPS = 1e-6
N_MOD = 6
DN_HEAD_DIM = 128
DN_CONV = 5
DN_CHUNK = 64
SG_CHUNK = 128
TOP_K = 6
N_EXPERT_GROUPS = 8
TOPK_GROUPS = 4
ROUTED_SCALE = 2.5

LANE = 128
BF16_SUBLANES = 16
MIB = 1024 * 1024


def _cparams(sem, vmem_mib):
    return pltpu.CompilerParams(dimension_semantics=sem, vmem_limit_bytes=vmem_mib * MIB)


def _col_tile(n, pref):
    t = min(pref, n)
    while n % t or t % LANE:
        t -= LANE
    return t


def _mod_row(tile_start, n_ctx_rows, seq):
    return jnp.where(tile_start < n_ctx_rows, 0, 1 + (tile_start - n_ctx_rows) // seq)


def _mod_spec(which, tm, n_ctx_rows, seq, d, ncol=None):
    if ncol is None:
        return pl.BlockSpec((None, None, 1, d),
                            lambda i, *_: (_mod_row(i * tm, n_ctx_rows, seq), which, 0, 0))
    return pl.BlockSpec((None, None, 1, ncol),
                        lambda i, j: (_mod_row(i * tm, n_ctx_rows, seq), which, 0, j))


def _ada_kernel(c_ref, w_ref, b_ref, o_ref):
    c = c_ref[...]
    a = (c * jax.nn.sigmoid(c)).astype(BF16)
    o_ref[...] = jnp.dot(a, w_ref[...].astype(BF16), preferred_element_type=F32) + b_ref[...]


def _ada(cvec, w_ada, b_ada):
    nl, d, n = w_ada.shape
    tn = min(512, n)
    return pl.pallas_call(
        _ada_kernel,
        grid=(nl, n // tn),
        in_specs=[pl.BlockSpec((8, d), lambda l, j: (0, 0)),
                  pl.BlockSpec((None, d, tn), lambda l, j: (l, 0, j)),
                  pl.BlockSpec((None, 1, tn), lambda l, j: (l, 0, j))],
        out_specs=pl.BlockSpec((None, 8, tn), lambda l, j: (l, 0, j)),
        out_shape=jax.ShapeDtypeStruct((nl, 8, n), F32),
        compiler_params=_cparams(("parallel", "parallel"), 40),
        name="ada",
    )(cvec, w_ada, b_ada.reshape(nl, 1, n))


def _in_proj_kernel(x_ref, sh_ref, sc_ref, nw_ref, w_ref, wba_ref, p_ref, pba_ref, h_ref):
    @pl.when(pl.program_id(1) == 0)
    def _():
        x = x_ref[...]
        ms = jnp.mean(x * x, axis=-1, keepdims=True)
        y = x * lax.rsqrt(ms + EPS) * nw_ref[...]
        hb = (y * (1.0 + sc_ref[...]) + sh_ref[...]).astype(BF16)
        h_ref[...] = hb
        pba_ref[...] = jnp.dot(hb, wba_ref[...], preferred_element_type=F32)

    p_ref[...] = jnp.dot(h_ref[...], w_ref[...], preferred_element_type=F32).astype(p_ref.dtype)


def _in_proj(x, mod, nw, w_main, w_ba, tm, n_ctx_rows, seq):
    r, d = x.shape
    n = w_main.shape[1]
    tn = _col_tile(n, 1024)
    return pl.pallas_call(
        _in_proj_kernel,
        grid=(r // tm, n // tn),
        in_specs=[pl.BlockSpec((tm, d), lambda i, j: (i, 0)),
                  _mod_spec(0, tm, n_ctx_rows, seq, d),
                  _mod_spec(1, tm, n_ctx_rows, seq, d),
                  pl.BlockSpec((1, d), lambda i, j: (0, 0)),
                  pl.BlockSpec((d, tn), lambda i, j: (0, j)),
                  pl.BlockSpec((d, LANE), lambda i, j: (0, 0))],
        out_specs=[pl.BlockSpec((tm, tn), lambda i, j: (i, j)),
                   pl.BlockSpec((tm, LANE), lambda i, j: (i, 0))],
        out_shape=[jax.ShapeDtypeStruct((r, n), BF16), jax.ShapeDtypeStruct((r, LANE), F32)],
        scratch_shapes=[pltpu.VMEM((tm, d), BF16)],
        compiler_params=_cparams(("parallel", "arbitrary"), 48),
        name="in_proj",
    )(x, mod, mod, nw, w_main, w_ba)


def _dn_prep_kernel(cur_ref, prev_ref, next_ref, cw_ref, pba_ref, alog_ref, dtb_ref,
                    qkv_ref, bg_ref, *, tc, seg_starts, seg_ends, dn_width, n_heads):
    i = pl.program_id(0)
    j = pl.program_id(1)
    cb = cur_ref.shape[1]
    start = i * tc
    is_start = functools.reduce(jnp.logical_or, [start == s for s in seg_starts])
    is_end = functools.reduce(jnp.logical_or, [start + tc == s for s in seg_ends])
    x = cur_ref[...].astype(F32)
    hp = prev_ref[...].astype(F32)
    hn = next_ref[...].astype(F32)
    nh = hp.shape[0]
    keep_p = jnp.where(is_start, 0.0, 1.0)
    keep_n = jnp.where(is_end, 0.0, 1.0)
    pm2 = hp[nh - 2:nh - 1] * keep_p
    pm1 = hp[nh - 1:nh] * keep_p
    np1 = hn[0:1] * keep_n
    np2 = hn[1:2] * keep_n
    row = lax.broadcasted_iota(jnp.int32, (tc, cb), 0)
    xm1 = jnp.where(row == 0, pm1, pltpu.roll(x, 1, axis=0))
    xm2 = jnp.where(row == 0, pm2, jnp.where(row == 1, pm1, pltpu.roll(x, 2, axis=0)))
    xp1 = jnp.where(row == tc - 1, np1, pltpu.roll(x, tc - 1, axis=0))
    xp2 = jnp.where(row == tc - 1, np2, jnp.where(row == tc - 2, np1, pltpu.roll(x, tc - 2, axis=0)))
    cw = cw_ref[...]
    y = cw[0:1] * xm2 + cw[1:2] * xm1 + cw[2:3] * x + cw[3:4] * xp1 + cw[4:5] * xp2
    y = y * jax.nn.sigmoid(y)
    kind = (j * cb) // dn_width
    qscale = jnp.where(kind == 0, DN_HEAD_DIM ** -0.5, 1.0)
    for hh in range(cb // DN_HEAD_DIM):
        seg = y[:, hh * DN_HEAD_DIM:(hh + 1) * DN_HEAD_DIM]
        ss = jnp.sum(seg * seg, axis=-1, keepdims=True)
        fac = jnp.where(kind == 2, 1.0, lax.rsqrt(ss + EPS) * qscale)
        qkv_ref[:, hh * DN_HEAD_DIM:(hh + 1) * DN_HEAD_DIM] = (seg * fac).astype(qkv_ref.dtype)

    @pl.when(j == 0)
    def _():
        p = pba_ref[...]
        lane = lax.broadcasted_iota(jnp.int32, p.shape, 1)
        beta = jax.nn.sigmoid(p)
        z = p + dtb_ref[...]
        sp = jnp.maximum(z, 0.0) + jnp.log(1.0 + jnp.exp(-jnp.abs(z)))
        g = -jnp.exp(alog_ref[...]) * sp
        bg_ref[...] = jnp.where(lane < 2 * n_heads, beta, g)


def _dn_prep(p_main, pba, conv_w, alog_row, dtb_row, tc, segs, dn_width, n_heads):
    r = p_main.shape[0]
    cb = min(512, dn_width)
    hb = BF16_SUBLANES
    nblk16 = r // hb
    seg_starts = tuple(s for s, _ in segs)
    seg_ends = tuple(s + n for s, n in segs)
    kern = functools.partial(_dn_prep_kernel, tc=tc, seg_starts=seg_starts, seg_ends=seg_ends,
                             dn_width=dn_width, n_heads=n_heads)
    return pl.pallas_call(
        kern,
        grid=(r // tc, 3 * dn_width // cb),
        in_specs=[pl.BlockSpec((tc, cb), lambda i, j: (i, j)),
                  pl.BlockSpec((hb, cb), lambda i, j: (jnp.maximum(i * (tc // hb) - 1, 0), j)),
                  pl.BlockSpec((hb, cb), lambda i, j: (jnp.minimum((i + 1) * (tc // hb), nblk16 - 1), j)),
                  pl.BlockSpec((DN_CONV, cb), lambda i, j: (0, j)),
                  pl.BlockSpec((tc, LANE), lambda i, j: (i, 0)),
                  pl.BlockSpec((1, LANE), lambda i, j: (0, 0)),
                  pl.BlockSpec((1, LANE), lambda i, j: (0, 0))],
        out_specs=[pl.BlockSpec((tc, cb), lambda i, j: (i, j)),
                   pl.BlockSpec((tc, LANE), lambda i, j: (i, 0))],
        out_shape=[jax.ShapeDtypeStruct((r, 3 * dn_width), BF16),
                   jax.ShapeDtypeStruct((r, LANE), F32)],
        compiler_params=_cparams(("parallel", "arbitrary"), 32),
        name="dn_prep",
    )(p_main, p_main, p_main, conv_w, pba, alog_row, dtb_row)


def _split3(a):
    a1 = a.astype(BF16)
    r1 = a - a1.astype(F32)
    a2 = r1.astype(BF16)
    a3 = (r1 - a2.astype(F32)).astype(BF16)
    return a1, a2, a3


def _dot(a, b):
    return jnp.dot(a, b, preferred_element_type=F32)


def _bdot(a, b):
    return jnp.einsum("hik,hkj->hij", a, b, preferred_element_type=F32)


def _dn_local_kernel(q_ref, k_ref, v_ref, kt_ref, bg_ref, gr_ref,
                     u_ref, w_ref, qg_ref, qk_ref, kdt_ref, eg_ref, *, nh):
    cs = DN_CHUNK
    hd = DN_HEAD_DIM
    row = lax.broadcasted_iota(jnp.int32, (cs, cs), 0)
    col = lax.broadcasted_iota(jnp.int32, (cs, cs), 1)
    bg = bg_ref[...]
    gr = gr_ref[...]
    gtot = jnp.sum(bg, axis=0, keepdims=True)
    heads = [slice(h * hd, (h + 1) * hd) for h in range(nh)]
    qs = jnp.stack([q_ref[:, hs] for hs in heads])
    ks = jnp.stack([k_ref[:, hs] for hs in heads])
    vs = jnp.stack([v_ref[:, hs] for hs in heads])
    kts = jnp.stack([kt_ref[hs, :] for hs in heads])
    kq = _bdot(jnp.concatenate([ks, qs], axis=1), kts)
    kk = kq[:, :cs]
    qk_raw = kq[:, cs:]
    qf = qs.astype(F32)
    kf = ks.astype(F32)
    vf = vs.astype(F32)
    ktf = kts.astype(F32)
    for d in range(2):
        rel = (row - col) if d == 0 else (col - row)
        incl = (rel >= 0)[None]
        strict = (rel > 0)[None]
        m_incl = jnp.where(rel >= 0, 1.0, 0.0).astype(BF16)
        m_incl_t = jnp.where(rel <= 0, 1.0, 0.0).astype(BF16)
        eye = jnp.where(rel == 0, 1.0, 0.0)[None]
        gc_c = sum(_dot(m_incl, part) for part in _split3(bg))
        gc_r = sum(_dot(part, m_incl_t) for part in _split3(gr[d * nh:(d + 1) * nh]))
        lane_b = d * nh
        lane_g = 2 * nh + d * nh
        gcc = jnp.stack([gc_c[:, lane_g + h:lane_g + h + 1] for h in range(nh)])
        bc = jnp.stack([bg[:, lane_b + h:lane_b + h + 1] for h in range(nh)])
        gcr = jnp.stack([gc_r[h:h + 1, :] for h in range(nh)])
        gt = jnp.stack([gtot[:, lane_g + h:lane_g + h + 1] for h in range(nh)])
        decay = jnp.where(incl, jnp.exp(jnp.where(incl, gcc - gcr, 0.0)), 0.0)
        lmat = jnp.where(strict, bc * kk * decay, 0.0)
        egc = jnp.exp(gcc)
        rhs = jnp.concatenate([vf * bc, kf * (bc * egc)], axis=2).astype(BF16)
        lp = lmat.astype(BF16)
        t = eye - lmat
        n_sq = 1
        while 2 * n_sq < cs:
            lp = _bdot(lp, lp).astype(BF16)
            t = t + _bdot(t.astype(BF16), lp)
            n_sq *= 2
        uw = _bdot(t.astype(BF16), rhs)
        qk_ref[d] = (qk_raw * decay).astype(qk_ref.dtype)
        qg = (qf * egc).astype(qg_ref.dtype)
        kdt = (ktf * jnp.exp(gt - gcr)).astype(kdt_ref.dtype)
        eg_ref[d] = jnp.broadcast_to(jnp.exp(gt), (nh, 1, LANE))
        for h, hs in enumerate(heads):
            u_ref[d, :, hs] = uw[h, :, :hd]
            w_ref[d, :, hs] = uw[h, :, hd:].astype(w_ref.dtype)
            qg_ref[d, :, hs] = qg[h]
            kdt_ref[d, hs, :] = kdt[h]


def _dn_local(qkv, kt, bg, grow, n_heads):
    r = qkv.shape[0]
    nc = r // DN_CHUNK
    width = n_heads * DN_HEAD_DIM
    kern = functools.partial(_dn_local_kernel, nh=n_heads)
    return pl.pallas_call(
        kern,
        grid=(nc,),
        in_specs=[pl.BlockSpec((DN_CHUNK, width), lambda c: (c, 0)),
                  pl.BlockSpec((DN_CHUNK, width), lambda c: (c, 1)),
                  pl.BlockSpec((DN_CHUNK, width), lambda c: (c, 2)),
                  pl.BlockSpec((None, width, DN_CHUNK), lambda c: (c, 0, 0)),
                  pl.BlockSpec((DN_CHUNK, LANE), lambda c: (c, 0)),
                  pl.BlockSpec((None, grow.shape[1], DN_CHUNK), lambda c: (c, 0, 0))],
        out_specs=[pl.BlockSpec((2, DN_CHUNK, width), lambda c: (0, c, 0)),
                   pl.BlockSpec((2, DN_CHUNK, width), lambda c: (0, c, 0)),
                   pl.BlockSpec((2, DN_CHUNK, width), lambda c: (0, c, 0)),
                   pl.BlockSpec((2, None, n_heads, DN_CHUNK, DN_CHUNK), lambda c: (0, c, 0, 0, 0)),
                   pl.BlockSpec((2, None, width, DN_CHUNK), lambda c: (0, c, 0, 0)),
                   pl.BlockSpec((2, None, n_heads, 1, LANE), lambda c: (0, c, 0, 0, 0))],
        out_shape=[jax.ShapeDtypeStruct((2, r, width), F32),
                   jax.ShapeDtypeStruct((2, r, width), BF16),
                   jax.ShapeDtypeStruct((2, r, width), BF16),
                   jax.ShapeDtypeStruct((2, nc, n_heads, DN_CHUNK, DN_CHUNK), BF16),
                   jax.ShapeDtypeStruct((2, nc, width, DN_CHUNK), BF16),
                   jax.ShapeDtypeStruct((2, nc, n_heads, 1, LANE), F32)],
        compiler_params=_cparams(("parallel",), 40),
        name="dn_local",
    )(qkv, qkv, qkv, kt, bg, grow)


def _dn_seq_kernel(*refs, nh):
    ins, (of_ref, ob_ref, s_ref) = refs[:12], refs[12:]
    cs = DN_CHUNK
    hd = DN_HEAD_DIM

    @pl.when(pl.program_id(1) == 0)
    def _():
        s_ref[...] = jnp.zeros_like(s_ref)

    heads = [slice(h * hd, (h + 1) * hd) for h in range(nh)]
    for d, o_ref in enumerate((of_ref, ob_ref)):
        u_ref, w_ref, qg_ref, qk_ref, kdt_ref, eg_ref = ins[6 * d:6 * d + 6]
        lhs = jnp.stack([jnp.concatenate([w_ref[:, hs], qg_ref[:, hs]], axis=0) for hs in heads])
        s_old = s_ref[d]
        ws = _bdot(lhs, s_old.astype(BF16))
        u = jnp.stack([u_ref[:, hs] for hs in heads])
        v_new = (u - ws[:, :cs]).astype(BF16)
        lhs2 = jnp.concatenate([qk_ref[...], jnp.stack([kdt_ref[hs, :] for hs in heads])], axis=1)
        r2 = _bdot(lhs2, v_new)
        o = ws[:, cs:] + r2[:, :cs]
        for h, hs in enumerate(heads):
            o_ref[:, hs] = o[h]
        s_ref[d] = s_old * eg_ref[...] + r2[:, cs:]


def _dn_seq(u, w, qg, qk, kdt, eg, n_batch, ctx_len, seq, n_heads):
    r = u.shape[1]
    ncx = ctx_len // DN_CHUNK
    nlt = seq // DN_CHUNK
    width = n_heads * DN_HEAD_DIM

    def chunk(d):
        def f(b, s):
            pos_ctx = s if d == 0 else ncx - 1 - s
            pos_lat = s - ncx if d == 0 else nlt - 1 - (s - ncx)
            return jnp.where(s < ncx, b * ncx + pos_ctx, n_batch * ncx + b * nlt + pos_lat)
        return f

    in_specs, args = [], []
    for d in range(2):
        ch = chunk(d)
        in_specs += [
            pl.BlockSpec((None, DN_CHUNK, width), lambda b, s, ch=ch, d=d: (d, ch(b, s), 0)),
            pl.BlockSpec((None, DN_CHUNK, width), lambda b, s, ch=ch, d=d: (d, ch(b, s), 0)),
            pl.BlockSpec((None, DN_CHUNK, width), lambda b, s, ch=ch, d=d: (d, ch(b, s), 0)),
            pl.BlockSpec((None, None, n_heads, DN_CHUNK, DN_CHUNK),
                         lambda b, s, ch=ch, d=d: (d, ch(b, s), 0, 0, 0)),
            pl.BlockSpec((None, None, width, DN_CHUNK), lambda b, s, ch=ch, d=d: (d, ch(b, s), 0, 0)),
            pl.BlockSpec((None, None, n_heads, 1, LANE), lambda b, s, ch=ch, d=d: (d, ch(b, s), 0, 0, 0)),
        ]
        args += [u, w, qg, qk, kdt, eg]
    kern = functools.partial(_dn_seq_kernel, nh=n_heads)
    return pl.pallas_call(
        kern,
        grid=(n_batch, ncx + nlt),
        in_specs=in_specs,
        out_specs=[pl.BlockSpec((DN_CHUNK, width), lambda b, s, ch=chunk(0): (ch(b, s), 0)),
                   pl.BlockSpec((DN_CHUNK, width), lambda b, s, ch=chunk(1): (ch(b, s), 0))],
        out_shape=[jax.ShapeDtypeStruct((r, width), F32), jax.ShapeDtypeStruct((r, width), F32)],
        scratch_shapes=[pltpu.VMEM((2, n_heads, DN_HEAD_DIM, DN_HEAD_DIM), F32)],
        compiler_params=_cparams(("parallel", "arbitrary"), 40),
        name="dn_seq",
    )(*args)


def _sg_kernel(pu_ref, pv_ref, lnw_ref, lnb_ref, sgw_ref, sgbt_ref, y_ref, *, n_groups):
    tr = pu_ref.shape[0]
    gd = pu_ref.shape[1] // n_groups
    v = jax.nn.gelu(pv_ref[...].astype(F32))
    mu = jnp.mean(v, axis=-1, keepdims=True)
    xc = v - mu
    var = jnp.mean(xc * xc, axis=-1, keepdims=True)
    vv = (xc * lax.rsqrt(var + EPS) * lnw_ref[...] + lnb_ref[...]).astype(BF16)
    sgbt = sgbt_ref[...]
    for ch in range(tr // SG_CHUNK):
        rs = slice(ch * SG_CHUNK, (ch + 1) * SG_CHUNK)
        for g in range(n_groups):
            gs = slice(g * gd, (g + 1) * gd)
            mixed = _dot(sgw_ref[g], vv[rs, gs]) + sgbt[:, g:g + 1]
            u = jax.nn.gelu(pu_ref[rs, gs].astype(F32))
            y_ref[rs, gs] = (u * mixed).astype(y_ref.dtype)


def _sg(p_main, lnw, lnb, sgw, sgbt, tr, u_blk, v_blk):
    r = p_main.shape[0]
    n_groups = sgw.shape[0]
    width = lnw.shape[1]
    kern = functools.partial(_sg_kernel, n_groups=n_groups)
    return pl.pallas_call(
        kern,
        grid=(r // tr,),
        in_specs=[pl.BlockSpec((tr, width), lambda i: (i, u_blk)),
                  pl.BlockSpec((tr, width), lambda i: (i, v_blk)),
                  pl.BlockSpec((1, width), lambda i: (0, 0)),
                  pl.BlockSpec((1, width), lambda i: (0, 0)),
                  pl.BlockSpec(sgw.shape, lambda i: (0, 0, 0)),
                  pl.BlockSpec(sgbt.shape, lambda i: (0, 0))],
        out_specs=pl.BlockSpec((tr, width), lambda i: (i, 0)),
        out_shape=jax.ShapeDtypeStruct((r, width), BF16),
        compiler_params=_cparams(("parallel",), 32),
        name="spatial_gating",
    )(p_main, p_main, lnw, lnb, sgw, sgbt)


def _merge_kernel(of_ref, ob_ref, z_ref, dnw_ref, ysg_ref, wdn_ref, wsg_ref, gdn_ref, gsg_ref,
                  m_ref, ydn_ref):
    @pl.when(pl.program_id(1) == 0)
    def _():
        nw = dnw_ref[...]
        for h in range(of_ref.shape[1] // DN_HEAD_DIM):
            hs = slice(h * DN_HEAD_DIM, (h + 1) * DN_HEAD_DIM)
            o = of_ref[:, hs] + ob_ref[:, hs]
            ms = jnp.mean(o * o, axis=-1, keepdims=True)
            z = z_ref[:, hs].astype(F32)
            ydn_ref[:, hs] = (o * lax.rsqrt(ms + EPS) * nw * (z * jax.nn.sigmoid(z))).astype(BF16)

    a = _dot(ydn_ref[...], wdn_ref[...])
    b = _dot(ysg_ref[...], wsg_ref[...])
    m = (jax.nn.sigmoid(gdn_ref[...].astype(F32)) * a + jax.nn.sigmoid(gsg_ref[...].astype(F32)) * b)
    m_ref[...] = m.astype(m_ref.dtype)


def _merge(o_f, o_b, p_main, dnw, ysg, wdn, wsg, tm, z_blk, gate_col0):
    r, width = ysg.shape
    d = wdn.shape[1]
    tn = min(512, d)
    g0 = gate_col0 // tn
    return pl.pallas_call(
        _merge_kernel,
        grid=(r // tm, d // tn),
        in_specs=[pl.BlockSpec((tm, width), lambda i, j: (i, 0)),
                  pl.BlockSpec((tm, width), lambda i, j: (i, 0)),
                  pl.BlockSpec((tm, width), lambda i, j: (i, z_blk)),
                  pl.BlockSpec((1, DN_HEAD_DIM), lambda i, j: (0, 0)),
                  pl.BlockSpec((tm, width), lambda i, j: (i, 0)),
                  pl.BlockSpec((width, tn), lambda i, j: (0, j)),
                  pl.BlockSpec((width, tn), lambda i, j: (0, j)),
                  pl.BlockSpec((tm, tn), lambda i, j: (i, g0 + j)),
                  pl.BlockSpec((tm, tn), lambda i, j: (i, g0 + d // tn + j))],
        out_specs=pl.BlockSpec((tm, tn), lambda i, j: (i, j)),
        out_shape=jax.ShapeDtypeStruct((r, d), BF16),
        scratch_shapes=[pltpu.VMEM((tm, width), BF16)],
        compiler_params=_cparams(("parallel", "arbitrary"), 48),
        name="merge",
    )(o_f, o_b, p_main, dnw, ysg, wdn, wsg, p_main, p_main)


def _out_proj_kernel(m_ref, w_ref, x_ref, ga_ref, o_ref):
    o_ref[...] = x_ref[...] + ga_ref[...] * _dot(m_ref[...], w_ref[...])


def _out_proj(m, w, x, mod, tm, n_ctx_rows, seq):
    r, d = x.shape
    tn = min(1024, d)
    return pl.pallas_call(
        _out_proj_kernel,
        grid=(r // tm, d // tn),
        in_specs=[pl.BlockSpec((tm, d), lambda i, j: (i, 0)),
                  pl.BlockSpec((d, tn), lambda i, j: (0, j)),
                  pl.BlockSpec((tm, tn), lambda i, j: (i, j)),
                  _mod_spec(2, tm, n_ctx_rows, seq, d, ncol=tn)],
        out_specs=pl.BlockSpec((tm, tn), lambda i, j: (i, j)),
        out_shape=jax.ShapeDtypeStruct((r, d), F32),
        input_output_aliases={2: 0},
        compiler_params=_cparams(("parallel", "arbitrary"), 48),
        name="out_proj",
    )(m, w, x, mod)


def _nt_dot(a, b):
    return lax.dot_general(a, b, (((1,), (1,)), ((), ())), preferred_element_type=F32)


def _route_kernel(x_ref, sh_ref, sc_ref, nw_ref, wrt_ref, bias_ref, h_ref, hf_ref, g_ref, eid_ref, wk_ref):
    x = x_ref[...]
    ms = jnp.mean(x * x, axis=-1, keepdims=True)
    h = x * lax.rsqrt(ms + EPS) * nw_ref[...] * (1.0 + sc_ref[...]) + sh_ref[...]
    hb = h.astype(BF16)
    h_ref[...] = hb
    hf_ref[...] = hb.astype(F32)
    logits = _nt_dot(wrt_ref[...].astype(BF16), hb)
    s = jax.nn.sigmoid(logits)
    sel = s + bias_ref[...]
    ne, tm = sel.shape
    gsz = ne // N_EXPERT_GROUPS
    sub = lax.broadcasted_iota(jnp.int32, (gsz, tm), 0)
    gs_rows = []
    for g in range(N_EXPERT_GROUPS):
        blk = sel[g * gsz:(g + 1) * gsz, :]
        m1 = jnp.max(blk, axis=0, keepdims=True)
        i1 = jnp.min(jnp.where(blk == m1, sub, gsz), axis=0, keepdims=True)
        m2 = jnp.max(jnp.where(sub == i1, -jnp.inf, blk), axis=0, keepdims=True)
        gs_rows.append(m1 + m2)
    masked_blocks = []
    for g in range(N_EXPERT_GROUPS):
        rank = jnp.zeros((1, tm), F32)
        for g2 in range(N_EXPERT_GROUPS):
            if g2 == g:
                continue
            ahead = (gs_rows[g2] > gs_rows[g]) if g2 > g else (gs_rows[g2] >= gs_rows[g])
            rank = rank + jnp.where(ahead, 1.0, 0.0)
        keep = rank < TOPK_GROUPS
        masked_blocks.append(jnp.where(keep, sel[g * gsz:(g + 1) * gsz, :], -jnp.inf))
    masked = jnp.concatenate(masked_blocks, axis=0)
    eidx = lax.broadcasted_iota(jnp.int32, (ne, tm), 0)
    rank = jnp.zeros((ne, tm), F32)
    for e2 in range(ne):
        r2 = masked[e2:e2 + 1, :]
        tie = jnp.where(eidx > e2, 1.0, 0.0)
        rank = rank + jnp.where(r2 > masked, 1.0, jnp.where(r2 == masked, tie, 0.0))
    wts = jnp.where(rank < TOP_K, s, 0.0)
    gates = wts / jnp.sum(wts, axis=0, keepdims=True) * ROUTED_SCALE
    g_ref[...] = gates
    ids, wks = [], []
    for k in range(eid_ref.shape[0]):
        hit = rank == k
        ids.append(jnp.sum(jnp.where(hit, eidx, 0), axis=0, keepdims=True))
        wks.append(jnp.sum(jnp.where(hit, gates, 0.0), axis=0, keepdims=True))
    eid_ref[...] = jnp.concatenate(ids, axis=0)
    wk_ref[...] = jnp.concatenate(wks, axis=0)


def _route(x, mod, nw, wrt, bias, tm, n_ctx_rows, seq):
    r, d = x.shape
    ne = wrt.shape[0]
    return pl.pallas_call(
        _route_kernel,
        grid=(r // tm,),
        in_specs=[pl.BlockSpec((tm, d), lambda i: (i, 0)),
                  _mod_spec(3, tm, n_ctx_rows, seq, d),
                  _mod_spec(4, tm, n_ctx_rows, seq, d),
                  pl.BlockSpec((1, d), lambda i: (0, 0)),
                  pl.BlockSpec((ne, d), lambda i: (0, 0)),
                  pl.BlockSpec((ne, 1), lambda i: (0, 0))],
        out_specs=[pl.BlockSpec((tm, d), lambda i: (i, 0)),
                   pl.BlockSpec((tm, d), lambda i: (i, 0)),
                   pl.BlockSpec((ne, tm), lambda i: (0, i)),
                   pl.BlockSpec((8, tm), lambda i: (0, i)),
                   pl.BlockSpec((8, tm), lambda i: (0, i))],
        out_shape=[jax.ShapeDtypeStruct((r, d), BF16), jax.ShapeDtypeStruct((r, d), F32),
                   jax.ShapeDtypeStruct((ne, r), F32), jax.ShapeDtypeStruct((8, r), jnp.int32),
                   jax.ShapeDtypeStruct((8, r), F32)],
        compiler_params=_cparams(("parallel",), 48),
        name="route",
    )(x, mod, mod, nw, wrt, bias)


def _ffn_kernel(h_ref, wg_ref, wu_ref, wd_ref, gate_ref, y_ref):
    @pl.when(pl.program_id(1) == 0)
    def _():
        y_ref[...] = jnp.zeros_like(y_ref)

    h = h_ref[...]
    a = _dot(h, wg_ref[...])
    u = _dot(h, wu_ref[...])
    hid = (a * jax.nn.sigmoid(a) * u * gate_ref[...]).astype(BF16)
    y_ref[...] += _dot(hid, wd_ref[...])


def _ffn(h, wg, wu, wd, gate, tm):
    r, d = h.shape
    ne, _, ff = wg.shape
    return pl.pallas_call(
        _ffn_kernel,
        grid=(r // tm, ne),
        in_specs=[pl.BlockSpec((tm, d), lambda i, e: (i, 0)),
                  pl.BlockSpec((None, d, ff), lambda i, e: (e, 0, 0)),
                  pl.BlockSpec((None, d, ff), lambda i, e: (e, 0, 0)),
                  pl.BlockSpec((None, ff, d), lambda i, e: (e, 0, 0)),
                  pl.BlockSpec((None, tm, 1), lambda i, e: (e, i, 0))],
        out_specs=pl.BlockSpec((tm, d), lambda i, e: (i, 0)),
        out_shape=jax.ShapeDtypeStruct((r, d), F32),
        compiler_params=_cparams(("parallel", "arbitrary"), 48),
        name="expert_ffn",
    )(h, wg, wu, wd, gate)


def _slot_wait(buf, sem, slot):
    pltpu.make_async_copy(buf.at[slot], buf.at[slot], sem.at[slot]).wait()


def _moe_group_kernel(tok_ref, texp_ref, h_hbm, wg_ref, wu_ref, wd_ref, gate_ref, y_ref, buf, sem,
                      *, tmx, nt):
    del texp_ref
    i = pl.program_id(0)

    def issue(tile, slot):
        def body(r, carry):
            t = tok_ref[tile * tmx + r]
            pltpu.make_async_copy(h_hbm.at[pl.ds(t, 1)], buf.at[slot, pl.ds(r, 1)], sem.at[slot]).start()
            return carry
        lax.fori_loop(0, tmx, body, 0)

    @pl.when(i == 0)
    def _():
        issue(0, 0)

    @pl.when(i + 1 < nt)
    def _():
        issue(i + 1, (i + 1) % 2)

    slot = i % 2
    _slot_wait(buf, sem, slot)
    x = buf[slot].astype(BF16)
    a = _dot(x, wg_ref[...])
    u = _dot(x, wu_ref[...])
    hid = (a * jax.nn.sigmoid(a) * u * gate_ref[...]).astype(BF16)
    y_ref[...] = _dot(hid, wd_ref[...])


def _moe_group(tok_pad, tile_exp, hf, wg, wu, wd, gate_pad, tmx):
    p = tok_pad.shape[0]
    nt = p // tmx
    d = hf.shape[1]
    ff = wg.shape[2]
    kern = functools.partial(_moe_group_kernel, tmx=tmx, nt=nt)
    grid_spec = pltpu.PrefetchScalarGridSpec(
        num_scalar_prefetch=2,
        grid=(nt,),
        in_specs=[pl.BlockSpec(memory_space=pl.ANY),
                  pl.BlockSpec((None, d, ff), lambda i, tok, te: (te[i], 0, 0)),
                  pl.BlockSpec((None, d, ff), lambda i, tok, te: (te[i], 0, 0)),
                  pl.BlockSpec((None, ff, d), lambda i, tok, te: (te[i], 0, 0)),
                  pl.BlockSpec((tmx, 1), lambda i, tok, te: (i, 0))],
        out_specs=pl.BlockSpec((tmx, d), lambda i, tok, te: (i, 0)),
        scratch_shapes=[pltpu.VMEM((2, tmx, d), F32), pltpu.SemaphoreType.DMA((2,))],
    )
    return pl.pallas_call(
        kern,
        grid_spec=grid_spec,
        out_shape=jax.ShapeDtypeStruct((p, d), F32),
        compiler_params=_cparams(("arbitrary",), 48),
        name="moe_group",
    )(tok_pad, tile_exp, hf, wg, wu, wd, gate_pad)


def _moe_combine_kernel(pos_ref, y_hbm, x_ref, ys_ref, ga_ref, *rest, tt, nk, nt, final):
    if final:
        fw_ref, o_ref, buf, sem = rest
    else:
        o_ref, buf, sem = rest
    i = pl.program_id(0)

    def issue(tile, slot):
        def body(r, carry):
            for k in range(nk):
                p = pos_ref[(tile * tt + r) * nk + k]
                pltpu.make_async_copy(y_hbm.at[pl.ds(p, 1)], buf.at[slot, k, pl.ds(r, 1)],
                                      sem.at[slot]).start()
            return carry
        lax.fori_loop(0, tt, body, 0)

    @pl.when(i == 0)
    def _():
        issue(0, 0)

    @pl.when(i + 1 < nt)
    def _():
        issue(i + 1, (i + 1) % 2)

    slot = i % 2
    _slot_wait(buf, sem, slot)
    routed = buf[slot, 0]
    for k in range(1, nk):
        routed = routed + buf[slot, k]
    x = x_ref[...] + ga_ref[...] * (ys_ref[...] + routed)
    if final:
        ms = jnp.mean(x * x, axis=-1, keepdims=True)
        x = x * lax.rsqrt(ms + EPS) * fw_ref[...]
    o_ref[...] = x


def _moe_combine(pos_flat, y_sorted, x, ys, mod, tt, n_ctx_rows, seq, final_w=None):
    r, d = x.shape
    nk = pos_flat.shape[0] // r
    nt = r // tt
    final = final_w is not None
    row_spec = pl.BlockSpec((tt, d), lambda i, pos: (i, 0))
    in_specs = [pl.BlockSpec(memory_space=pl.ANY), row_spec, row_spec,
                _mod_spec(5, tt, n_ctx_rows, seq, d)]
    args = [pos_flat, y_sorted, x, ys, mod]
    if final:
        in_specs.append(pl.BlockSpec((1, d), lambda i, pos: (0, 0)))
        args.append(final_w)
    kern = functools.partial(_moe_combine_kernel, tt=tt, nk=nk, nt=nt, final=final)
    grid_spec = pltpu.PrefetchScalarGridSpec(
        num_scalar_prefetch=1,
        grid=(nt,),
        in_specs=in_specs,
        out_specs=row_spec,
        scratch_shapes=[pltpu.VMEM((2, nk, tt, d), F32), pltpu.SemaphoreType.DMA((2,))],
    )
    return pl.pallas_call(
        kern,
        grid_spec=grid_spec,
        out_shape=jax.ShapeDtypeStruct((r, d), F32),
        compiler_params=_cparams(("arbitrary",), 48),
        name="moe_combine",
    )(*args)


def _moe_plan(gates_t, eid_t, wk_t, tmx, p_rows):
    ne, r = gates_t.shape
    eid = eid_t[:TOP_K].T
    wk = wk_t[:TOP_K].T
    sel = (gates_t > 0).astype(jnp.int32)
    pos_in_e = jnp.cumsum(sel, axis=1) - 1
    cnt = pos_in_e[:, -1] + 1
    padded = ((cnt + tmx - 1) // tmx) * tmx
    ends = jnp.cumsum(padded)
    gstart = ends - padded
    cstart = jnp.cumsum(cnt) - cnt
    pos = gstart[eid] + jnp.take_along_axis(pos_in_e.T, eid, axis=1)
    pos_flat = pos.reshape(-1)
    tok_flat = jnp.repeat(jnp.arange(r, dtype=jnp.int32), TOP_K)
    _, tok_s, gate_s = lax.sort((pos_flat, tok_flat, wk.reshape(-1)), num_keys=1)
    nt = p_rows // tmx
    tile_exp = jnp.minimum(jnp.sum(ends[None, :] <= (jnp.arange(nt) * tmx)[:, None], axis=1), ne - 1)
    tile_exp = tile_exp.astype(jnp.int32)
    e_p = jnp.repeat(tile_exp, tmx)
    j = jnp.arange(p_rows, dtype=jnp.int32) - gstart[e_p]
    valid = j < cnt[e_p]
    ci = jnp.clip(cstart[e_p] + j, 0, r * TOP_K - 1)
    tok_pad = jnp.where(valid, tok_s[ci], 0).astype(jnp.int32)
    gate_pad = jnp.where(valid, gate_s[ci], 0.0)[:, None]
    return tok_pad, gate_pad, tile_exp, pos_flat.astype(jnp.int32)


def kernel(x, c, ctx, c_ctx, w_ada, b_ada, norm1_w, w_in, dn_conv_w, dn_a_log, dn_dt_bias, dn_norm_w, sg_ln_w, sg_ln_b, sg_w, sg_b, w_br_dn, w_br_sg, w_out, norm2_w, w_router, router_bias, w_exp_gate, w_exp_up, w_exp_down, w_sh_gate, w_sh_up, w_sh_down, final_norm_w):
    nb, seq, d = x.shape
    ctx_len = ctx.shape[1]
    depth = w_ada.shape[0]
    n_heads = dn_a_log.shape[-1]
    dn_width = dn_conv_w.shape[-1] // 3
    sg_width = sg_ln_w.shape[-1]
    ne, _, ff = w_exp_gate.shape[1:]
    sh_ff = w_sh_gate.shape[-1]
    n_ctx_rows = nb * ctx_len
    r = n_ctx_rows + nb * seq
    assert nb + 1 <= 8 and dn_width == n_heads * DN_HEAD_DIM and sh_ff % ff == 0
    tm = min(512, n_ctx_rows)
    tc = min(256, ctx_len)
    assert n_ctx_rows % tm == 0 and seq % tm == 0 and ctx_len % tc == 0 and seq % tc == 0
    tmx = 256
    tt = 64
    p_rows = -(-(r * TOP_K) // tmx) * tmx + ne * tmx
    segs =[(b * ctx_len, ctx_len) for b in range(nb)] + [(n_ctx_rows + b * seq, seq) for b in range(nb)]

    n_ba = 4 * n_heads
    c_qkv = 3 * dn_width
    z_col = c_qkv
    u_col = z_col + dn_width
    v_col = u_col + sg_width
    gate_col = v_col + sg_width
    assert dn_width == sg_width and z_col % dn_width == 0 and gate_col % 512 == 0

    cvec = jnp.concatenate([c_ctx[None, :], c, jnp.zeros((8 - 1 - nb, d), F32)], axis=0)
    mod_all = _ada(cvec, w_ada, b_ada)
    xa = jnp.concatenate([ctx.reshape(n_ctx_rows, d), x.reshape(nb * seq, d)], axis=0)

    out = None
    for l in range(depth):
        last = l == depth - 1
        mod = mod_all[l, :nb + 1].reshape(nb + 1, N_MOD, 1, d)
        w_main = jnp.concatenate([w_in[l][:, :c_qkv], w_in[l][:, c_qkv + n_ba:]], axis=1).astype(BF16)
        w_ba = jnp.pad(w_in[l][:, c_qkv:c_qkv + n_ba], ((0, 0), (0, LANE - n_ba))).astype(BF16)

        p_main, pba = _in_proj(xa, mod, norm1_w[l][None, :], w_main, w_ba, tm, n_ctx_rows, seq)
        alog_row = jnp.pad(dn_a_log[l].reshape(1, -1), ((0, 0), (2 * n_heads, LANE - 4 * n_heads)))
        dtb_row = jnp.pad(dn_dt_bias[l].reshape(1, -1), ((0, 0), (2 * n_heads, LANE - 4 * n_heads)))
        qkv, bg = _dn_prep(p_main, pba, dn_conv_w[l], alog_row, dtb_row, tc, segs, dn_width, n_heads)
        grow = bg[:, 2 * n_heads:4 * n_heads].reshape(r // DN_CHUNK, DN_CHUNK, 2 * n_heads).transpose(0, 2, 1)
        grow = jnp.pad(grow, ((0, 0), (0, max(0, BF16_SUBLANES - 2 * n_heads)), (0, 0)))
        kt = qkv[:, dn_width:2 * dn_width].reshape(r // DN_CHUNK, DN_CHUNK, dn_width).transpose(0, 2, 1)
        o_f, o_b = _dn_seq(*_dn_local(qkv, kt, bg, grow, n_heads), nb, ctx_len, seq, n_heads)
        ysg = _sg(p_main, sg_ln_w[l][None, :], sg_ln_b[l][None, :], sg_w[l].astype(BF16), sg_b[l].T,
                  min(256, ctx_len), u_col // sg_width, v_col // sg_width)
        m = _merge(o_f, o_b, p_main, dn_norm_w[l][None, :], ysg, w_br_dn[l].astype(BF16),
                   w_br_sg[l].astype(BF16), tm, z_col // dn_width, gate_col)
        xa = _out_proj(m, w_out[l].astype(BF16), xa, mod, tm, n_ctx_rows, seq)

        h2, h2f, gates_t, eid_t, wk_t = _route(xa, mod, norm2_w[l][None, :], w_router[l].T,
                                               router_bias[l][:, None], min(256, tm), n_ctx_rows, seq)
        tok_pad, gate_pad, tile_exp, pos_flat = _moe_plan(gates_t, eid_t, wk_t, tmx, p_rows)
        y_sorted = _moe_group(tok_pad, tile_exp, h2f, w_exp_gate[l].astype(BF16), w_exp_up[l].astype(BF16),
                              w_exp_down[l].astype(BF16), gate_pad, tmx)
        nsh = sh_ff // ff
        y_shared = _ffn(h2,
                        w_sh_gate[l].reshape(d, nsh, ff).transpose(1, 0, 2).astype(BF16),
                        w_sh_up[l].reshape(d, nsh, ff).transpose(1, 0, 2).astype(BF16),
                        w_sh_down[l].reshape(nsh, ff, d).astype(BF16),
                        jnp.ones((nsh, r, 1), F32), tm)
        if last:
            out = _moe_combine(pos_flat, y_sorted, xa, y_shared, mod, tt, n_ctx_rows, seq, final_norm_w[None, :])
        else:
            xa = _moe_combine(pos_flat, y_sorted, xa, y_shared, mod, tt, n_ctx_rows, seq)
    return out[n_ctx_rows:].reshape(nb, seq, d)
```

```python
import functools

import jax
import jax.numpy as jnp
from jax import lax
from jax.experimental import pallas as pl
from jax.experimental.pallas import tpu as pltpu

F32 = jnp.float32
BF16 = jnp.bfloat16

EPS = 1e-6
N_MOD = 6
DN_HEAD_DIM = 128
DN_CONV = 5
DN_CHUNK = 64
SG_CHUNK = 128
TOP_K = 6
N_EXPERT_GROUPS = 8
TOPK_GROUPS = 4
ROUTED_SCALE = 2.5

LANE = 128
BF16_SUBLANES = 16
DMA_THREADS = 2
MIB = 1024 * 1024


def _cparams(sem, vmem_mib):
    return pltpu.CompilerParams(dimension_semantics=sem, vmem_limit_bytes=vmem_mib * MIB)


def _col_tile(n, pref):
    t = min(pref, n)
    while n % t or t % LANE:
        t -= LANE
    return t


def _mod_row(tile_start, n_ctx_rows, seq):
    return jnp.where(tile_start < n_ctx_rows, 0, 1 + (tile_start - n_ctx_rows) // seq)


def _mod_spec(which, tm, n_ctx_rows, seq, d, ncol=None, tile0=0):
    if ncol is None:
        return pl.BlockSpec((None, None, 1, d),
                            lambda i, *_: (_mod_row((i + tile0) * tm, n_ctx_rows, seq), which, 0, 0))
    return pl.BlockSpec((None, None, 1, ncol),
                        lambda i, j: (_mod_row(i * tm, n_ctx_rows, seq), which, 0, j))


def _ada_kernel(c_ref, w_ref, b_ref, o_ref):
    c = c_ref[...]
    a = (c * jax.nn.sigmoid(c)).astype(BF16)
    o_ref[...] = jnp.dot(a, w_ref[...].astype(BF16), preferred_element_type=F32) + b_ref[...]


def _ada(cvec, w_ada, b_ada):
    nl, d, n = w_ada.shape
    tn = min(512, n)
    return pl.pallas_call(
        _ada_kernel,
        grid=(nl, n // tn),
        in_specs=[pl.BlockSpec((8, d), lambda l, j: (0, 0)),
                  pl.BlockSpec((None, d, tn), lambda l, j: (l, 0, j)),
                  pl.BlockSpec((None, 1, tn), lambda l, j: (l, 0, j))],
        out_specs=pl.BlockSpec((None, 8, tn), lambda l, j: (l, 0, j)),
        out_shape=jax.ShapeDtypeStruct((nl, 8, n), F32),
        compiler_params=_cparams(("parallel", "parallel"), 40),
        name="ada",
    )(cvec, w_ada, b_ada.reshape(nl, 1, n))


def _in_proj_kernel(x_ref, sh_ref, sc_ref, nw_ref, w_ref, wba_ref, p_ref, pba_ref, h_ref):
    @pl.when(pl.program_id(1) == 0)
    def _():
        x = x_ref[...]
        ms = jnp.mean(x * x, axis=-1, keepdims=True)
        y = x * lax.rsqrt(ms + EPS) * nw_ref[...]
        hb = (y * (1.0 + sc_ref[...]) + sh_ref[...]).astype(BF16)
        h_ref[...] = hb
        pba_ref[...] = jnp.dot(hb, wba_ref[...], preferred_element_type=F32)

    p_ref[...] = jnp.dot(h_ref[...], w_ref[...], preferred_element_type=F32).astype(p_ref.dtype)


def _in_proj(x, mod, nw, w_main, w_ba, layer, tm, n_ctx_rows, seq):
    r, d = x.shape
    n = w_main.shape[2]
    tn = _col_tile(n, 1024)
    return pl.pallas_call(
        _in_proj_kernel,
        grid=(r // tm, n // tn),
        in_specs=[pl.BlockSpec((tm, d), lambda i, j: (i, 0)),
                  _mod_spec(0, tm, n_ctx_rows, seq, d),
                  _mod_spec(1, tm, n_ctx_rows, seq, d),
                  pl.BlockSpec((1, d), lambda i, j: (0, 0)),
                  pl.BlockSpec((None, d, tn), lambda i, j: (layer, 0, j)),
                  pl.BlockSpec((None, d, LANE), lambda i, j: (layer, 0, 0))],
        out_specs=[pl.BlockSpec((tm, tn), lambda i, j: (i, j)),
                   pl.BlockSpec((tm, LANE), lambda i, j: (i, 0))],
        out_shape=[jax.ShapeDtypeStruct((r, n), BF16), jax.ShapeDtypeStruct((r, LANE), F32)],
        scratch_shapes=[pltpu.VMEM((tm, d), BF16)],
        compiler_params=_cparams(("parallel", "arbitrary"), 48),
        name="in_proj",
    )(x, mod, mod, nw, w_main, w_ba)


def _dn_prep_kernel(cur_ref, prev_ref, next_ref, cw_ref, pba_ref, alog_ref, dtb_ref,
                    qkv_ref, bg_ref, *, tc, seg_starts, seg_ends, dn_width, n_heads):
    i = pl.program_id(0)
    j = pl.program_id(1)
    cb = cur_ref.shape[1]
    start = i * tc
    is_start = functools.reduce(jnp.logical_or, [start == s for s in seg_starts])
    is_end = functools.reduce(jnp.logical_or, [start + tc == s for s in seg_ends])
    x = cur_ref[...].astype(F32)
    hp = prev_ref[...].astype(F32)
    hn = next_ref[...].astype(F32)
    nh = hp.shape[0]
    keep_p = jnp.where(is_start, 0.0, 1.0)
    keep_n = jnp.where(is_end, 0.0, 1.0)
    pm2 = hp[nh - 2:nh - 1] * keep_p
    pm1 = hp[nh - 1:nh] * keep_p
    np1 = hn[0:1] * keep_n
    np2 = hn[1:2] * keep_n
    row = lax.broadcasted_iota(jnp.int32, (tc, cb), 0)
    xm1 = jnp.where(row == 0, pm1, pltpu.roll(x, 1, axis=0))
    xm2 = jnp.where(row == 0, pm2, jnp.where(row == 1, pm1, pltpu.roll(x, 2, axis=0)))
    xp1 = jnp.where(row == tc - 1, np1, pltpu.roll(x, tc - 1, axis=0))
    xp2 = jnp.where(row == tc - 1, np2, jnp.where(row == tc - 2, np1, pltpu.roll(x, tc - 2, axis=0)))
    cw = cw_ref[...]
    y = cw[0:1] * xm2 + cw[1:2] * xm1 + cw[2:3] * x + cw[3:4] * xp1 + cw[4:5] * xp2
    y = y * jax.nn.sigmoid(y)
    kind = (j * cb) // dn_width
    qscale = jnp.where(kind == 0, DN_HEAD_DIM ** -0.5, 1.0)
    for hh in range(cb // DN_HEAD_DIM):
        seg = y[:, hh * DN_HEAD_DIM:(hh + 1) * DN_HEAD_DIM]
        ss = jnp.sum(seg * seg, axis=-1, keepdims=True)
        fac = jnp.where(kind == 2, 1.0, lax.rsqrt(ss + EPS) * qscale)
        qkv_ref[:, hh * DN_HEAD_DIM:(hh + 1) * DN_HEAD_DIM] = (seg * fac).astype(qkv_ref.dtype)

    @pl.when(j == 0)
    def _():
        p = pba_ref[...]
        lane = lax.broadcasted_iota(jnp.int32, p.shape, 1)
        beta = jax.nn.sigmoid(p)
        z = p + dtb_ref[...]
        sp = jnp.maximum(z, 0.0) + jnp.log(1.0 + jnp.exp(-jnp.abs(z)))
        g = -jnp.exp(alog_ref[...]) * sp
        bg_ref[...] = jnp.where(lane < 2 * n_heads, beta, g)


def _dn_prep(p_main, pba, conv_w, alog_row, dtb_row, tc, segs, dn_width, n_heads):
    r = p_main.shape[0]
    cb = min(512, dn_width)
    hb = BF16_SUBLANES
    nblk16 = r // hb
    seg_starts = tuple(s for s, _ in segs)
    seg_ends = tuple(s + n for s, n in segs)
    kern = functools.partial(_dn_prep_kernel, tc=tc, seg_starts=seg_starts, seg_ends=seg_ends,
                             dn_width=dn_width, n_heads=n_heads)
    return pl.pallas_call(
        kern,
        grid=(r // tc, 3 * dn_width // cb),
        in_specs=[pl.BlockSpec((tc, cb), lambda i, j: (i, j)),
                  pl.BlockSpec((hb, cb), lambda i, j: (jnp.maximum(i * (tc // hb) - 1, 0), j)),
                  pl.BlockSpec((hb, cb), lambda i, j: (jnp.minimum((i + 1) * (tc // hb), nblk16 - 1), j)),
                  pl.BlockSpec((DN_CONV, cb), lambda i, j: (0, j)),
                  pl.BlockSpec((tc, LANE), lambda i, j: (i, 0)),
                  pl.BlockSpec((1, LANE), lambda i, j: (0, 0)),
                  pl.BlockSpec((1, LANE), lambda i, j: (0, 0))],
        out_specs=[pl.BlockSpec((tc, cb), lambda i, j: (i, j)),
                   pl.BlockSpec((tc, LANE), lambda i, j: (i, 0))],
        out_shape=[jax.ShapeDtypeStruct((r, 3 * dn_width), BF16),
                   jax.ShapeDtypeStruct((r, LANE), F32)],
        compiler_params=_cparams(("parallel", "arbitrary"), 32),
        name="dn_prep",
    )(p_main, p_main, p_main, conv_w, pba, alog_row, dtb_row)


def _split3(a):
    a1 = a.astype(BF16)
    r1 = a - a1.astype(F32)
    a2 = r1.astype(BF16)
    a3 = (r1 - a2.astype(F32)).astype(BF16)
    return a1, a2, a3


def _dot(a, b):
    return jnp.dot(a, b, preferred_element_type=F32)


def _bdot(a, b):
    return jnp.einsum("hik,hkj->hij", a, b, preferred_element_type=F32)


def _dn_local_kernel(q_ref, k_ref, v_ref, kt_ref, bg_ref, gr_ref,
                     u_ref, w_ref, qg_ref, qk_ref, kdt_ref, eg_ref, *, nh):
    cs = DN_CHUNK
    hd = DN_HEAD_DIM
    row = lax.broadcasted_iota(jnp.int32, (cs, cs), 0)
    col = lax.broadcasted_iota(jnp.int32, (cs, cs), 1)
    bg = bg_ref[...]
    gr = gr_ref[...]
    gtot = jnp.sum(bg, axis=0, keepdims=True)
    heads = [slice(h * hd, (h + 1) * hd) for h in range(nh)]
    qs = jnp.stack([q_ref[:, hs] for hs in heads])
    ks = jnp.stack([k_ref[:, hs] for hs in heads])
    vs = jnp.stack([v_ref[:, hs] for hs in heads])
    kts = jnp.stack([kt_ref[hs, :] for hs in heads])
    kq = _bdot(jnp.concatenate([ks, qs], axis=1), kts)
    kk = kq[:, :cs]
    qk_raw = kq[:, cs:]
    qf = qs.astype(F32)
    kf = ks.astype(F32)
    vf = vs.astype(F32)
    ktf = kts.astype(F32)
    for d in range(2):
        rel = (row - col) if d == 0 else (col - row)
        incl = (rel >= 0)[None]
        strict = (rel > 0)[None]
        m_incl = jnp.where(rel >= 0, 1.0, 0.0).astype(BF16)
        m_incl_t = jnp.where(rel <= 0, 1.0, 0.0).astype(BF16)
        eye = jnp.where(rel == 0, 1.0, 0.0)[None]
        gc_c = sum(_dot(m_incl, part) for part in _split3(bg))
        gc_r = sum(_dot(part, m_incl_t) for part in _split3(gr[d * nh:(d + 1) * nh]))
        lane_b = d * nh
        lane_g = 2 * nh + d * nh
        gcc = jnp.stack([gc_c[:, lane_g + h:lane_g + h + 1] for h in range(nh)])
        bc = jnp.stack([bg[:, lane_b + h:lane_b + h + 1] for h in range(nh)])
        gcr = jnp.stack([gc_r[h:h + 1, :] for h in range(nh)])
        gt = jnp.stack([gtot[:, lane_g + h:lane_g + h + 1] for h in range(nh)])
        decay = jnp.where(incl, jnp.exp(jnp.where(incl, gcc - gcr, 0.0)), 0.0)
        lmat = jnp.where(strict, bc * kk * decay, 0.0)
        egc = jnp.exp(gcc)
        rhs = jnp.concatenate([vf * bc, kf * (bc * egc)], axis=2).astype(BF16)
        lp = lmat.astype(BF16)
        t = eye - lmat
        n_sq = 1
        while 2 * n_sq < cs:
            lp = _bdot(lp, lp).astype(BF16)
            t = t + _bdot(t.astype(BF16), lp)
            n_sq *= 2
        uw = _bdot(t.astype(BF16), rhs)
        qk_ref[d] = (qk_raw * decay).astype(qk_ref.dtype)
        qg = (qf * egc).astype(qg_ref.dtype)
        kdt = (ktf * jnp.exp(gt - gcr)).astype(kdt_ref.dtype)
        eg_ref[d] = jnp.broadcast_to(jnp.exp(gt), (nh, 1, LANE))
        for h, hs in enumerate(heads):
            u_ref[d, :, hs] = uw[h, :, :hd]
            w_ref[d, :, hs] = uw[h, :, hd:].astype(w_ref.dtype)
            qg_ref[d, :, hs] = qg[h]
            kdt_ref[d, hs, :] = kdt[h]


def _dn_local(qkv, kt, bg, grow, n_heads):
    r = qkv.shape[0]
    nc = r // DN_CHUNK
    width = n_heads * DN_HEAD_DIM
    kern = functools.partial(_dn_local_kernel, nh=n_heads)
    return pl.pallas_call(
        kern,
        grid=(nc,),
        in_specs=[pl.BlockSpec((DN_CHUNK, width), lambda c: (c, 0)),
                  pl.BlockSpec((DN_CHUNK, width), lambda c: (c, 1)),
                  pl.BlockSpec((DN_CHUNK, width), lambda c: (c, 2)),
                  pl.BlockSpec((None, width, DN_CHUNK), lambda c: (c, 0, 0)),
                  pl.BlockSpec((DN_CHUNK, LANE), lambda c: (c, 0)),
                  pl.BlockSpec((None, grow.shape[1], DN_CHUNK), lambda c: (c, 0, 0))],
        out_specs=[pl.BlockSpec((2, DN_CHUNK, width), lambda c: (0, c, 0)),
                   pl.BlockSpec((2, DN_CHUNK, width), lambda c: (0, c, 0)),
                   pl.BlockSpec((2, DN_CHUNK, width), lambda c: (0, c, 0)),
                   pl.BlockSpec((2, None, n_heads, DN_CHUNK, DN_CHUNK), lambda c: (0, c, 0, 0, 0)),
                   pl.BlockSpec((2, None, width, DN_CHUNK), lambda c: (0, c, 0, 0)),
                   pl.BlockSpec((2, None, n_heads, 1, LANE), lambda c: (0, c, 0, 0, 0))],
        out_shape=[jax.ShapeDtypeStruct((2, r, width), F32),
                   jax.ShapeDtypeStruct((2, r, width), BF16),
                   jax.ShapeDtypeStruct((2, r, width), BF16),
                   jax.ShapeDtypeStruct((2, nc, n_heads, DN_CHUNK, DN_CHUNK), BF16),
                   jax.ShapeDtypeStruct((2, nc, width, DN_CHUNK), BF16),
                   jax.ShapeDtypeStruct((2, nc, n_heads, 1, LANE), F32)],
        compiler_params=_cparams(("parallel",), 40),
        name="dn_local",
    )(qkv, qkv, qkv, kt, bg, grow)


def _dn_seq_kernel(*refs, nh):
    ins, (of_ref, ob_ref, s_ref) = refs[:12], refs[12:]
    cs = DN_CHUNK
    hd = DN_HEAD_DIM

    @pl.when(pl.program_id(1) == 0)
    def _():
        s_ref[...] = jnp.zeros_like(s_ref)

    heads = [slice(h * hd, (h + 1) * hd) for h in range(nh)]
    for d, o_ref in enumerate((of_ref, ob_ref)):
        u_ref, w_ref, qg_ref, qk_ref, kdt_ref, eg_ref = ins[6 * d:6 * d + 6]
        lhs = jnp.stack([jnp.concatenate([w_ref[:, hs], qg_ref[:, hs]], axis=0) for hs in heads])
        s_old = s_ref[d]
        ws = _bdot(lhs, s_old.astype(BF16))
        u = jnp.stack([u_ref[:, hs] for hs in heads])
        v_new = (u - ws[:, :cs]).astype(BF16)
        lhs2 = jnp.concatenate([qk_ref[...], jnp.stack([kdt_ref[hs, :] for hs in heads])], axis=1)
        r2 = _bdot(lhs2, v_new)
        o = ws[:, cs:] + r2[:, :cs]
        for h, hs in enumerate(heads):
            o_ref[:, hs] = o[h]
        s_ref[d] = s_old * eg_ref[...] + r2[:, cs:]


def _dn_seq(u, w, qg, qk, kdt, eg, n_batch, ctx_len, seq, n_heads):
    r = u.shape[1]
    ncx = ctx_len // DN_CHUNK
    nlt = seq // DN_CHUNK
    width = n_heads * DN_HEAD_DIM

    def chunk(d):
        def f(b, s):
            pos_ctx = s if d == 0 else ncx - 1 - s
            pos_lat = s - ncx if d == 0 else nlt - 1 - (s - ncx)
            return jnp.where(s < ncx, b * ncx + pos_ctx, n_batch * ncx + b * nlt + pos_lat)
        return f

    in_specs, args = [], []
    for d in range(2):
        ch = chunk(d)
        in_specs += [
            pl.BlockSpec((None, DN_CHUNK, width), lambda b, s, ch=ch, d=d: (d, ch(b, s), 0)),
            pl.BlockSpec((None, DN_CHUNK, width), lambda b, s, ch=ch, d=d: (d, ch(b, s), 0)),
            pl.BlockSpec((None, DN_CHUNK, width), lambda b, s, ch=ch, d=d: (d, ch(b, s), 0)),
            pl.BlockSpec((None, None, n_heads, DN_CHUNK, DN_CHUNK),
                         lambda b, s, ch=ch, d=d: (d, ch(b, s), 0, 0, 0)),
            pl.BlockSpec((None, None, width, DN_CHUNK), lambda b, s, ch=ch, d=d: (d, ch(b, s), 0, 0)),
            pl.BlockSpec((None, None, n_heads, 1, LANE), lambda b, s, ch=ch, d=d: (d, ch(b, s), 0, 0, 0)),
        ]
        args += [u, w, qg, qk, kdt, eg]
    kern = functools.partial(_dn_seq_kernel, nh=n_heads)
    return pl.pallas_call(
        kern,
        grid=(n_batch, ncx + nlt),
        in_specs=in_specs,
        out_specs=[pl.BlockSpec((DN_CHUNK, width), lambda b, s, ch=chunk(0): (ch(b, s), 0)),
                   pl.BlockSpec((DN_CHUNK, width), lambda b, s, ch=chunk(1): (ch(b, s), 0))],
        out_shape=[jax.ShapeDtypeStruct((r, width), F32), jax.ShapeDtypeStruct((r, width), F32)],
        scratch_shapes=[pltpu.VMEM((2, n_heads, DN_HEAD_DIM, DN_HEAD_DIM), F32)],
        compiler_params=_cparams(("parallel", "arbitrary"), 40),
        name="dn_seq",
    )(*args)


def _sg_kernel(pu_ref, pv_ref, lnw_ref, lnb_ref, sgw_ref, sgbt_ref, y_ref, *, n_groups):
    tr = pu_ref.shape[0]
    gd = pu_ref.shape[1] // n_groups
    v = jax.nn.gelu(pv_ref[...].astype(F32))
    mu = jnp.mean(v, axis=-1, keepdims=True)
    xc = v - mu
    var = jnp.mean(xc * xc, axis=-1, keepdims=True)
    vv = (xc * lax.rsqrt(var + EPS) * lnw_ref[...] + lnb_ref[...]).astype(BF16)
    sgbt = sgbt_ref[...]
    for ch in range(tr // SG_CHUNK):
        rs = slice(ch * SG_CHUNK, (ch + 1) * SG_CHUNK)
        for g in range(n_groups):
            gs = slice(g * gd, (g + 1) * gd)
            mixed = _dot(sgw_ref[g], vv[rs, gs]) + sgbt[:, g:g + 1]
            u = jax.nn.gelu(pu_ref[rs, gs].astype(F32))
            y_ref[rs, gs] = (u * mixed).astype(y_ref.dtype)


def _sg(p_main, lnw, lnb, sgw, sgbt, layer, tr, u_blk, v_blk):
    r = p_main.shape[0]
    n_groups = sgw.shape[1]
    width = lnw.shape[1]
    kern = functools.partial(_sg_kernel, n_groups=n_groups)
    return pl.pallas_call(
        kern,
        grid=(r // tr,),
        in_specs=[pl.BlockSpec((tr, width), lambda i: (i, u_blk)),
                  pl.BlockSpec((tr, width), lambda i: (i, v_blk)),
                  pl.BlockSpec((1, width), lambda i: (0, 0)),
                  pl.BlockSpec((1, width), lambda i: (0, 0)),
                  pl.BlockSpec((None,) + sgw.shape[1:], lambda i: (layer, 0, 0, 0)),
                  pl.BlockSpec(sgbt.shape, lambda i: (0, 0))],
        out_specs=pl.BlockSpec((tr, width), lambda i: (i, 0)),
        out_shape=jax.ShapeDtypeStruct((r, width), BF16),
        compiler_params=_cparams(("parallel",), 32),
        name="spatial_gating",
    )(p_main, p_main, lnw, lnb, sgw, sgbt)


def _merge_kernel(of_ref, ob_ref, z_ref, dnw_ref, ysg_ref, wdn_ref, wsg_ref, gdn_ref, gsg_ref,
                  m_ref, ydn_ref):
    @pl.when(pl.program_id(1) == 0)
    def _():
        nw = dnw_ref[...]
        for h in range(of_ref.shape[1] // DN_HEAD_DIM):
            hs = slice(h * DN_HEAD_DIM, (h + 1) * DN_HEAD_DIM)
            o = of_ref[:, hs] + ob_ref[:, hs]
            ms = jnp.mean(o * o, axis=-1, keepdims=True)
            z = z_ref[:, hs].astype(F32)
            ydn_ref[:, hs] = (o * lax.rsqrt(ms + EPS) * nw * (z * jax.nn.sigmoid(z))).astype(BF16)

    a = _dot(ydn_ref[...], wdn_ref[...])
    b = _dot(ysg_ref[...], wsg_ref[...])
    m = (jax.nn.sigmoid(gdn_ref[...].astype(F32)) * a + jax.nn.sigmoid(gsg_ref[...].astype(F32)) * b)
    m_ref[...] = m.astype(m_ref.dtype)


def _merge(o_f, o_b, p_main, dnw, ysg, wdn, wsg, layer, tm, z_blk, gate_col0):
    r, width = ysg.shape
    d = wdn.shape[2]
    tn = min(512, d)
    g0 = gate_col0 // tn
    return pl.pallas_call(
        _merge_kernel,
        grid=(r // tm, d // tn),
        in_specs=[pl.BlockSpec((tm, width), lambda i, j: (i, 0)),
                  pl.BlockSpec((tm, width), lambda i, j: (i, 0)),
                  pl.BlockSpec((tm, width), lambda i, j: (i, z_blk)),
                  pl.BlockSpec((1, DN_HEAD_DIM), lambda i, j: (0, 0)),
                  pl.BlockSpec((tm, width), lambda i, j: (i, 0)),
                  pl.BlockSpec((None, width, tn), lambda i, j: (layer, 0, j)),
                  pl.BlockSpec((None, width, tn), lambda i, j: (layer, 0, j)),
                  pl.BlockSpec((tm, tn), lambda i, j: (i, g0 + j)),
                  pl.BlockSpec((tm, tn), lambda i, j: (i, g0 + d // tn + j))],
        out_specs=pl.BlockSpec((tm, tn), lambda i, j: (i, j)),
        out_shape=jax.ShapeDtypeStruct((r, d), BF16),
        scratch_shapes=[pltpu.VMEM((tm, width), BF16)],
        compiler_params=_cparams(("parallel", "arbitrary"), 48),
        name="merge",
    )(o_f, o_b, p_main, dnw, ysg, wdn, wsg, p_main, p_main)


def _out_proj_kernel(m_ref, w_ref, x_ref, ga_ref, o_ref):
    o_ref[...] = x_ref[...] + ga_ref[...] * _dot(m_ref[...], w_ref[...])


def _out_proj(m, w, layer, x, mod, tm, n_ctx_rows, seq):
    r, d = x.shape
    tn = min(1024, d)
    return pl.pallas_call(
        _out_proj_kernel,
        grid=(r // tm, d // tn),
        in_specs=[pl.BlockSpec((tm, d), lambda i, j: (i, 0)),
                  pl.BlockSpec((None, d, tn), lambda i, j: (layer, 0, j)),
                  pl.BlockSpec((tm, tn), lambda i, j: (i, j)),
                  _mod_spec(2, tm, n_ctx_rows, seq, d, ncol=tn)],
        out_specs=pl.BlockSpec((tm, tn), lambda i, j: (i, j)),
        out_shape=jax.ShapeDtypeStruct((r, d), F32),
        input_output_aliases={2: 0},
        compiler_params=_cparams(("parallel", "arbitrary"), 48),
        name="out_proj",
    )(m, w, x, mod)


def _nt_dot(a, b):
    return lax.dot_general(a, b, (((1,), (1,)), ((), ())), preferred_element_type=F32)


def _route_kernel(x_ref, sh_ref, sc_ref, nw_ref, wrt_ref, bias_ref, h_ref, hf_ref, g_ref, eid_ref, wk_ref):
    x = x_ref[...]
    ms = jnp.mean(x * x, axis=-1, keepdims=True)
    h = x * lax.rsqrt(ms + EPS) * nw_ref[...] * (1.0 + sc_ref[...]) + sh_ref[...]
    hb = h.astype(BF16)
    h_ref[...] = hb
    hf_ref[...] = hb.astype(F32)
    logits = _nt_dot(wrt_ref[...].astype(BF16), hb)
    s = jax.nn.sigmoid(logits)
    sel = s + bias_ref[...]
    ne, tm = sel.shape
    gsz = ne // N_EXPERT_GROUPS
    sub = lax.broadcasted_iota(jnp.int32, (gsz, tm), 0)
    gs_rows = []
    for g in range(N_EXPERT_GROUPS):
        blk = sel[g * gsz:(g + 1) * gsz, :]
        m1 = jnp.max(blk, axis=0, keepdims=True)
        i1 = jnp.min(jnp.where(blk == m1, sub, gsz), axis=0, keepdims=True)
        m2 = jnp.max(jnp.where(sub == i1, -jnp.inf, blk), axis=0, keepdims=True)
        gs_rows.append(m1 + m2)
    masked_blocks = []
    for g in range(N_EXPERT_GROUPS):
        rank = jnp.zeros((1, tm), F32)
        for g2 in range(N_EXPERT_GROUPS):
            if g2 == g:
                continue
            ahead = (gs_rows[g2] > gs_rows[g]) if g2 > g else (gs_rows[g2] >= gs_rows[g])
            rank = rank + jnp.where(ahead, 1.0, 0.0)
        keep = rank < TOPK_GROUPS
        masked_blocks.append(jnp.where(keep, sel[g * gsz:(g + 1) * gsz, :], -jnp.inf))
    masked = jnp.concatenate(masked_blocks, axis=0)
    eidx = lax.broadcasted_iota(jnp.int32, (ne, tm), 0)
    rank = jnp.zeros((ne, tm), F32)
    for e2 in range(ne):
        r2 = masked[e2:e2 + 1, :]
        tie = jnp.where(eidx > e2, 1.0, 0.0)
        rank = rank + jnp.where(r2 > masked, 1.0, jnp.where(r2 == masked, tie, 0.0))
    wts = jnp.where(rank < TOP_K, s, 0.0)
    gates = wts / jnp.sum(wts, axis=0, keepdims=True) * ROUTED_SCALE
    g_ref[...] = gates
    ids, wks = [], []
    for k in range(eid_ref.shape[0]):
        hit = rank == k
        ids.append(jnp.sum(jnp.where(hit, eidx, 0), axis=0, keepdims=True))
        wks.append(jnp.sum(jnp.where(hit, gates, 0.0), axis=0, keepdims=True))
    eid_ref[...] = jnp.concatenate(ids, axis=0)
    wk_ref[...] = jnp.concatenate(wks, axis=0)


def _route(x, mod, nw, wrt, bias, tm, n_ctx_rows, seq):
    r, d = x.shape
    ne = wrt.shape[0]
    return pl.pallas_call(
        _route_kernel,
        grid=(r // tm,),
        in_specs=[pl.BlockSpec((tm, d), lambda i: (i, 0)),
                  _mod_spec(3, tm, n_ctx_rows, seq, d),
                  _mod_spec(4, tm, n_ctx_rows, seq, d),
                  pl.BlockSpec((1, d), lambda i: (0, 0)),
                  pl.BlockSpec((ne, d), lambda i: (0, 0)),
                  pl.BlockSpec((ne, 1), lambda i: (0, 0))],
        out_specs=[pl.BlockSpec((tm, d), lambda i: (i, 0)),
                   pl.BlockSpec((tm, d), lambda i: (i, 0)),
                   pl.BlockSpec((ne, tm), lambda i: (0, i)),
                   pl.BlockSpec((8, tm), lambda i: (0, i)),
                   pl.BlockSpec((8, tm), lambda i: (0, i))],
        out_shape=[jax.ShapeDtypeStruct((r, d), BF16), jax.ShapeDtypeStruct((r, d), F32),
                   jax.ShapeDtypeStruct((ne, r), F32), jax.ShapeDtypeStruct((8, r), jnp.int32),
                   jax.ShapeDtypeStruct((8, r), F32)],
        compiler_params=_cparams(("parallel",), 48),
        name="route",
    )(x, mod, mod, nw, wrt, bias)


def _shared_ffn_kernel(h_ref, wg_ref, wu_ref, wd_ref, y_ref):
    @pl.when(pl.program_id(1) == 0)
    def _():
        y_ref[...] = jnp.zeros_like(y_ref)

    h = h_ref[...]
    a = _dot(h, wg_ref[...])
    u = _dot(h, wu_ref[...])
    y_ref[...] += _dot((a * jax.nn.sigmoid(a) * u).astype(BF16), wd_ref[...])


def _shared_ffn(h, wg, wu, wd, layer, ff, tm):
    r, d = h.shape
    return pl.pallas_call(
        _shared_ffn_kernel,
        grid=(r // tm, wg.shape[2] // ff),
        in_specs=[pl.BlockSpec((tm, d), lambda i, e: (i, 0)),
                  pl.BlockSpec((None, d, ff), lambda i, e: (layer, 0, e)),
                  pl.BlockSpec((None, d, ff), lambda i, e: (layer, 0, e)),
                  pl.BlockSpec((None, ff, d), lambda i, e: (layer, e, 0))],
        out_specs=pl.BlockSpec((tm, d), lambda i, e: (i, 0)),
        out_shape=jax.ShapeDtypeStruct((r, d), F32),
        compiler_params=_cparams(("parallel", "arbitrary"), 48),
        name="shared_ffn",
    )(h, wg, wu, wd)


def _slot_wait(buf, sem, slot):
    pltpu.make_async_copy(buf.at[slot], buf.at[slot], sem.at[slot]).wait()


def _moe_group_kernel(tok_ref, texp_ref, h_hbm, wg_ref, wu_ref, wd_ref, gate_ref, y_ref, buf, sem,
                      *, tmx, nt):
    del texp_ref
    i = pl.program_id(0)

    def issue(tile, slot):
        def body(r2, carry):
            for par in range(DMA_THREADS):
                r = r2 * DMA_THREADS + par
                t = tok_ref[tile * tmx + r]
                pltpu.make_async_copy(h_hbm.at[pl.ds(t, 1)], buf.at[slot, pl.ds(r, 1)],
                                      sem.at[slot]).start(priority=par)
            return carry
        lax.fori_loop(0, tmx // DMA_THREADS, body, 0)

    @pl.when(i == 0)
    def _():
        issue(0, 0)

    @pl.when(i + 1 < nt)
    def _():
        issue(i + 1, (i + 1) % 2)

    slot = i % 2
    _slot_wait(buf, sem, slot)
    x = buf[slot].astype(BF16)
    a = _dot(x, wg_ref[...])
    u = _dot(x, wu_ref[...])
    hid = (a * jax.nn.sigmoid(a) * u * gate_ref[...]).astype(BF16)
    y_ref[...] = _dot(hid, wd_ref[...])


def _moe_group(tok_pad, tile_exp, hf, wg, wu, wd, layer, gate_pad, tmx):
    p = tok_pad.shape[0]
    nt = p // tmx
    d = hf.shape[1]
    ff = wg.shape[3]
    kern = functools.partial(_moe_group_kernel, tmx=tmx, nt=nt)
    grid_spec = pltpu.PrefetchScalarGridSpec(
        num_scalar_prefetch=2,
        grid=(nt,),
        in_specs=[pl.BlockSpec(memory_space=pl.ANY),
                  pl.BlockSpec((None, None, d, ff), lambda i, tok, te: (layer, te[i], 0, 0)),
                  pl.BlockSpec((None, None, d, ff), lambda i, tok, te: (layer, te[i], 0, 0)),
                  pl.BlockSpec((None, None, ff, d), lambda i, tok, te: (layer, te[i], 0, 0)),
                  pl.BlockSpec((tmx, 1), lambda i, tok, te: (i, 0))],
        out_specs=pl.BlockSpec((tmx, d), lambda i, tok, te: (i, 0)),
        scratch_shapes=[pltpu.VMEM((2, tmx, d), F32), pltpu.SemaphoreType.DMA((2,))],
    )
    return pl.pallas_call(
        kern,
        grid_spec=grid_spec,
        out_shape=jax.ShapeDtypeStruct((p, d), F32),
        compiler_params=_cparams(("arbitrary",), 48),
        name="moe_group",
    )(tok_pad, tile_exp, hf, wg, wu, wd, gate_pad)


def _moe_combine_kernel(pos_ref, y_hbm, x_ref, ys_ref, ga_ref, *rest, tt, nk, nt, t0, final):
    if final:
        fw_ref, o_ref, buf, sem = rest
    else:
        o_ref, buf, sem = rest
    i = pl.program_id(0)

    def issue(tile, slot):
        def body(r, carry):
            for k in range(nk):
                p = pos_ref[((tile + t0) * tt + r) * nk + k]
                pltpu.make_async_copy(y_hbm.at[pl.ds(p, 1)], buf.at[slot, k, pl.ds(r, 1)],
                                      sem.at[slot]).start(priority=k % DMA_THREADS)
            return carry
        lax.fori_loop(0, tt, body, 0)

    @pl.when(i == 0)
    def _():
        issue(0, 0)

    @pl.when(i + 1 < nt)
    def _():
        issue(i + 1, (i + 1) % 2)

    slot = i % 2
    _slot_wait(buf, sem, slot)
    routed = buf[slot, 0]
    for k in range(1, nk):
        routed = routed + buf[slot, k]
    x = x_ref[...] + ga_ref[...] * (ys_ref[...] + routed)
    if final:
        ms = jnp.mean(x * x, axis=-1, keepdims=True)
        x = x * lax.rsqrt(ms + EPS) * fw_ref[...]
    o_ref[...] = x


def _moe_combine(pos_flat, y_sorted, x, ys, mod, tt, n_ctx_rows, seq, final_w=None):
    r, d = x.shape
    nk = pos_flat.shape[0] // r
    final = final_w is not None
    t0 = n_ctx_rows // tt if final else 0
    nt = r // tt - t0
    row_spec = pl.BlockSpec((tt, d), lambda i, pos: (i + t0, 0))
    in_specs = [pl.BlockSpec(memory_space=pl.ANY), row_spec, row_spec,
                _mod_spec(5, tt, n_ctx_rows, seq, d, tile0=t0)]
    args = [pos_flat, y_sorted, x, ys, mod]
    if final:
        in_specs.append(pl.BlockSpec((1, d), lambda i, pos: (0, 0)))
        args.append(final_w)
    kern = functools.partial(_moe_combine_kernel, tt=tt, nk=nk, nt=nt, t0=t0, final=final)
    grid_spec = pltpu.PrefetchScalarGridSpec(
        num_scalar_prefetch=1,
        grid=(nt,),
        in_specs=in_specs,
        out_specs=pl.BlockSpec((tt, d), lambda i, pos: (i, 0)),
        scratch_shapes=[pltpu.VMEM((2, nk, tt, d), F32), pltpu.SemaphoreType.DMA((2,))],
    )
    return pl.pallas_call(
        kern,
        grid_spec=grid_spec,
        out_shape=jax.ShapeDtypeStruct((nt * tt, d), F32),
        compiler_params=_cparams(("arbitrary",), 48),
        name="moe_combine",
    )(*args)


def _moe_plan(gates_t, eid_t, wk_t, tmx, p_rows):
    ne, r = gates_t.shape
    eid = eid_t[:TOP_K].T
    wk = wk_t[:TOP_K].T
    sel = (gates_t > 0).astype(jnp.int32)
    pos_in_e = jnp.cumsum(sel, axis=1) - 1
    cnt = pos_in_e[:, -1] + 1
    padded = ((cnt + tmx - 1) // tmx) * tmx
    ends = jnp.cumsum(padded)
    gstart = ends - padded
    cstart = jnp.cumsum(cnt) - cnt
    pos = gstart[eid] + jnp.take_along_axis(pos_in_e.T, eid, axis=1)
    pos_flat = pos.reshape(-1)
    tok_flat = jnp.repeat(jnp.arange(r, dtype=jnp.int32), TOP_K)
    _, tok_s, gate_s = lax.sort((pos_flat, tok_flat, wk.reshape(-1)), num_keys=1)
    nt = p_rows // tmx
    tile_exp = jnp.minimum(jnp.sum(ends[None, :] <= (jnp.arange(nt) * tmx)[:, None], axis=1), ne - 1)
    tile_exp = tile_exp.astype(jnp.int32)
    j = (jnp.arange(nt, dtype=jnp.int32) * tmx - gstart[tile_exp])[:, None] + jnp.arange(tmx, dtype=jnp.int32)[None, :]
    valid = j < cnt[tile_exp][:, None]
    ci = jnp.clip(cstart[tile_exp][:, None] + j, 0, r * TOP_K - 1).reshape(-1)
    valid = valid.reshape(-1)
    tok_pad = jnp.where(valid, tok_s[ci], 0).astype(jnp.int32)
    gate_pad = jnp.where(valid, gate_s[ci], 0.0)[:, None]
    return tok_pad, gate_pad, tile_exp, pos_flat.astype(jnp.int32)


def kernel(x, c, ctx, c_ctx, w_ada, b_ada, norm1_w, w_in, dn_conv_w, dn_a_log, dn_dt_bias, dn_norm_w, sg_ln_w, sg_ln_b, sg_w, sg_b, w_br_dn, w_br_sg, w_out, norm2_w, w_router, router_bias, w_exp_gate, w_exp_up, w_exp_down, w_sh_gate, w_sh_up, w_sh_down, final_norm_w):
    nb, seq, d = x.shape
    ctx_len = ctx.shape[1]
    depth = w_ada.shape[0]
    n_heads = dn_a_log.shape[-1]
    dn_width = dn_conv_w.shape[-1] // 3
    sg_width = sg_ln_w.shape[-1]
    ne, _, ff = w_exp_gate.shape[1:]
    sh_ff = w_sh_gate.shape[-1]
    n_ctx_rows = nb * ctx_len
    r = n_ctx_rows + nb * seq
    assert nb + 1 <= 8 and dn_width == n_heads * DN_HEAD_DIM and sh_ff % ff == 0
    tm = min(512, n_ctx_rows)
    tc = min(256, ctx_len)
    assert n_ctx_rows % tm == 0 and seq % tm == 0 and ctx_len % tc == 0 and seq % tc == 0
    tmx = 256
    tt = 64
    p_rows = -(-(r * TOP_K) // tmx) * tmx + ne * tmx
    segs =[(b * ctx_len, ctx_len) for b in range(nb)] + [(n_ctx_rows + b * seq, seq) for b in range(nb)]

    n_ba = 4 * n_heads
    c_qkv = 3 * dn_width
    z_col = c_qkv
    u_col = z_col + dn_width
    v_col = u_col + sg_width
    gate_col = v_col + sg_width
    assert dn_width == sg_width and z_col % dn_width == 0 and gate_col % 512 == 0

    cvec = jnp.concatenate([c_ctx[None, :], c, jnp.zeros((8 - 1 - nb, d), F32)], axis=0)
    mod_all = _ada(cvec, w_ada, b_ada)
    xa = jnp.concatenate([ctx.reshape(n_ctx_rows, d), x.reshape(nb * seq, d)], axis=0)

    w_main = jnp.concatenate([w_in[:, :, :c_qkv], w_in[:, :, c_qkv + n_ba:]], axis=2).astype(BF16)
    w_ba = jnp.pad(w_in[:, :, c_qkv:c_qkv + n_ba], ((0, 0), (0, 0), (0, LANE - n_ba))).astype(BF16)
    sgw_b, wdn_b, wsg_b, wout_b = (a.astype(BF16) for a in (sg_w, w_br_dn, w_br_sg, w_out))
    weg_b, weu_b, wed_b = (a.astype(BF16) for a in (w_exp_gate, w_exp_up, w_exp_down))
    wsg_sh_b, wsu_sh_b, wsd_sh_b = (a.astype(BF16) for a in (w_sh_gate, w_sh_up, w_sh_down))
    pad_ba = ((0, 0), (2 * n_heads, LANE - 4 * n_heads))

    out = None
    for l in range(depth):
        last = l == depth - 1
        mod = mod_all[l, :nb + 1].reshape(nb + 1, N_MOD, 1, d)

        p_main, pba = _in_proj(xa, mod, norm1_w[l][None, :], w_main, w_ba, l, tm, n_ctx_rows, seq)
        qkv, bg = _dn_prep(p_main, pba, dn_conv_w[l], jnp.pad(dn_a_log[l].reshape(1, -1), pad_ba),
                           jnp.pad(dn_dt_bias[l].reshape(1, -1), pad_ba), tc, segs, dn_width, n_heads)
        grow = bg[:, 2 * n_heads:4 * n_heads].reshape(r // DN_CHUNK, DN_CHUNK, 2 * n_heads).transpose(0, 2, 1)
        grow = jnp.pad(grow, ((0, 0), (0, max(0, BF16_SUBLANES - 2 * n_heads)), (0, 0)))
        kt = qkv[:, dn_width:2 * dn_width].reshape(r // DN_CHUNK, DN_CHUNK, dn_width).transpose(0, 2, 1)
        o_f, o_b = _dn_seq(*_dn_local(qkv, kt, bg, grow, n_heads), nb, ctx_len, seq, n_heads)
        ysg = _sg(p_main, sg_ln_w[l][None, :], sg_ln_b[l][None, :], sgw_b, sg_b[l].T, l,
                  min(256, ctx_len), u_col // sg_width, v_col // sg_width)
        m = _merge(o_f, o_b, p_main, dn_norm_w[l][None, :], ysg, wdn_b, wsg_b, l, tm, z_col // dn_width, gate_col)
        xa = _out_proj(m, wout_b, l, xa, mod, tm, n_ctx_rows, seq)

        h2, h2f, gates_t, eid_t, wk_t = _route(xa, mod, norm2_w[l][None, :], w_router[l].T,
                                               router_bias[l][:, None], min(256, tm), n_ctx_rows, seq)
        tok_pad, gate_pad, tile_exp, pos_flat = _moe_plan(gates_t, eid_t, wk_t, tmx, p_rows)
        y_sorted = _moe_group(tok_pad, tile_exp, h2f, weg_b, weu_b, wed_b, l, gate_pad, tmx)
        y_shared = _shared_ffn(h2, wsg_sh_b, wsu_sh_b, wsd_sh_b, l, ff, tm)
        if last:
            out = _moe_combine(pos_flat, y_sorted, xa, y_shared, mod, tt, n_ctx_rows, seq, final_norm_w[None, :])
        else:
            xa = _moe_combine(pos_flat, y_sorted, xa, y_shared, mod, tt, n_ctx_rows, seq)
    return out.reshape(nb, seq, d)
```

```python
import functools

import jax
import jax.numpy as jnp
from jax import lax
from jax.experimental import pallas as pl
from jax.experimental.pallas import tpu as pltpu

F32 = jnp.float32
BF16 = jnp.bfloat16

EPS = 1e-6
N_MOD = 6
DN_HEAD_DIM = 128
DN_CONV = 5
DN_CHUNK = 64
SG_CHUNK = 128
TOP_K = 6
N_EXPERT_GROUPS = 8
TOPK_GROUPS = 4
ROUTED_SCALE = 2.5

LANE = 128
BF16_SUBLANES = 16
DMA_THREADS = 2
ISSUE_UNROLL = 8
MIB = 1024 * 1024


def _cparams(sem, vmem_mib):
    return pltpu.CompilerParams(dimension_semantics=sem, vmem_limit_bytes=vmem_mib * MIB)


def _col_tile(n, pref):
    t = min(pref, n)
    while n % t or t % LANE:
        t -= LANE
    return t


def _mod_row(tile_start, n_ctx_rows, seq):
    return jnp.where(tile_start < n_ctx_rows, 0, 1 + (tile_start - n_ctx_rows) // seq)


def _mod_spec(which, tm, n_ctx_rows, seq, d, ncol=None, tile0=0):
    if ncol is None:
        return pl.BlockSpec((None, None, 1, d),
                            lambda i, *_: (_mod_row((i + tile0) * tm, n_ctx_rows, seq), which, 0, 0))
    return pl.BlockSpec((None, None, 1, ncol),
                        lambda i, j: (_mod_row(i * tm, n_ctx_rows, seq), which, 0, j))


def _ada_kernel(c_ref, w_ref, b_ref, o_ref):
    c = c_ref[...]
    a = (c * jax.nn.sigmoid(c)).astype(BF16)
    o_ref[...] = jnp.dot(a, w_ref[...].astype(BF16), preferred_element_type=F32) + b_ref[...]


def _ada(cvec, w_ada, b_ada):
    nl, d, n = w_ada.shape
    tn = min(512, n)
    return pl.pallas_call(
        _ada_kernel,
        grid=(nl, n // tn),
        in_specs=[pl.BlockSpec((8, d), lambda l, j: (0, 0)),
                  pl.BlockSpec((None, d, tn), lambda l, j: (l, 0, j)),
                  pl.BlockSpec((None, 1, tn), lambda l, j: (l, 0, j))],
        out_specs=pl.BlockSpec((None, 8, tn), lambda l, j: (l, 0, j)),
        out_shape=jax.ShapeDtypeStruct((nl, 8, n), F32),
        compiler_params=_cparams(("parallel", "parallel"), 40),
        name="ada",
    )(cvec, w_ada, b_ada.reshape(nl, 1, n))


def _in_proj_kernel(x_ref, sh_ref, sc_ref, nw_ref, w_ref, wba_ref, p_ref, pba_ref, h_ref):
    @pl.when(pl.program_id(1) == 0)
    def _():
        x = x_ref[...]
        ms = jnp.mean(x * x, axis=-1, keepdims=True)
        y = x * lax.rsqrt(ms + EPS) * nw_ref[...]
        hb = (y * (1.0 + sc_ref[...]) + sh_ref[...]).astype(BF16)
        h_ref[...] = hb
        pba_ref[...] = jnp.dot(hb, wba_ref[...], preferred_element_type=F32)

    p_ref[...] = jnp.dot(h_ref[...], w_ref[...], preferred_element_type=F32).astype(p_ref.dtype)


def _in_proj(x, mod, nw, w_main, w_ba, layer, tm, n_ctx_rows, seq):
    r, d = x.shape
    n = w_main.shape[2]
    tn = _col_tile(n, 1024)
    return pl.pallas_call(
        _in_proj_kernel,
        grid=(r // tm, n // tn),
        in_specs=[pl.BlockSpec((tm, d), lambda i, j: (i, 0)),
                  _mod_spec(0, tm, n_ctx_rows, seq, d),
                  _mod_spec(1, tm, n_ctx_rows, seq, d),
                  pl.BlockSpec((1, d), lambda i, j: (0, 0)),
                  pl.BlockSpec((None, d, tn), lambda i, j: (layer, 0, j)),
                  pl.BlockSpec((None, d, LANE), lambda i, j: (layer, 0, 0))],
        out_specs=[pl.BlockSpec((tm, tn), lambda i, j: (i, j)),
                   pl.BlockSpec((tm, LANE), lambda i, j: (i, 0))],
        out_shape=[jax.ShapeDtypeStruct((r, n), BF16), jax.ShapeDtypeStruct((r, LANE), F32)],
        scratch_shapes=[pltpu.VMEM((tm, d), BF16)],
        compiler_params=_cparams(("parallel", "arbitrary"), 48),
        name="in_proj",
    )(x, mod, mod, nw, w_main, w_ba)


def _dn_prep_kernel(cur_ref, prev_ref, next_ref, cw_ref, pba_ref, alog_ref, dtb_ref,
                    qkv_ref, bg_ref, *, tc, seg_starts, seg_ends, dn_width, n_heads):
    i = pl.program_id(0)
    j = pl.program_id(1)
    cb = cur_ref.shape[1]
    start = i * tc
    is_start = functools.reduce(jnp.logical_or, [start == s for s in seg_starts])
    is_end = functools.reduce(jnp.logical_or, [start + tc == s for s in seg_ends])
    x = cur_ref[...].astype(F32)
    hp = prev_ref[...].astype(F32)
    hn = next_ref[...].astype(F32)
    nh = hp.shape[0]
    keep_p = jnp.where(is_start, 0.0, 1.0)
    keep_n = jnp.where(is_end, 0.0, 1.0)
    pm2 = hp[nh - 2:nh - 1] * keep_p
    pm1 = hp[nh - 1:nh] * keep_p
    np1 = hn[0:1] * keep_n
    np2 = hn[1:2] * keep_n
    row = lax.broadcasted_iota(jnp.int32, (tc, cb), 0)
    xm1 = jnp.where(row == 0, pm1, pltpu.roll(x, 1, axis=0))
    xm2 = jnp.where(row == 0, pm2, jnp.where(row == 1, pm1, pltpu.roll(x, 2, axis=0)))
    xp1 = jnp.where(row == tc - 1, np1, pltpu.roll(x, tc - 1, axis=0))
    xp2 = jnp.where(row == tc - 1, np2, jnp.where(row == tc - 2, np1, pltpu.roll(x, tc - 2, axis=0)))
    cw = cw_ref[...]
    y = cw[0:1] * xm2 + cw[1:2] * xm1 + cw[2:3] * x + cw[3:4] * xp1 + cw[4:5] * xp2
    y = y * jax.nn.sigmoid(y)
    kind = (j * cb) // dn_width
    qscale = jnp.where(kind == 0, DN_HEAD_DIM ** -0.5, 1.0)
    for hh in range(cb // DN_HEAD_DIM):
        seg = y[:, hh * DN_HEAD_DIM:(hh + 1) * DN_HEAD_DIM]
        ss = jnp.sum(seg * seg, axis=-1, keepdims=True)
        fac = jnp.where(kind == 2, 1.0, lax.rsqrt(ss + EPS) * qscale)
        qkv_ref[:, hh * DN_HEAD_DIM:(hh + 1) * DN_HEAD_DIM] = (seg * fac).astype(qkv_ref.dtype)

    @pl.when(j == 0)
    def _():
        p = pba_ref[...]
        lane = lax.broadcasted_iota(jnp.int32, p.shape, 1)
        beta = jax.nn.sigmoid(p)
        z = p + dtb_ref[...]
        sp = jnp.maximum(z, 0.0) + jnp.log(1.0 + jnp.exp(-jnp.abs(z)))
        g = -jnp.exp(alog_ref[...]) * sp
        bg_ref[...] = jnp.where(lane < 2 * n_heads, beta, g)


def _dn_prep(p_main, pba, conv_w, alog_row, dtb_row, tc, segs, dn_width, n_heads):
    r = p_main.shape[0]
    cb = min(1024, dn_width)
    hb = BF16_SUBLANES
    nblk16 = r // hb
    seg_starts = tuple(s for s, _ in segs)
    seg_ends = tuple(s + n for s, n in segs)
    kern = functools.partial(_dn_prep_kernel, tc=tc, seg_starts=seg_starts, seg_ends=seg_ends,
                             dn_width=dn_width, n_heads=n_heads)
    return pl.pallas_call(
        kern,
        grid=(r // tc, 3 * dn_width // cb),
        in_specs=[pl.BlockSpec((tc, cb), lambda i, j: (i, j)),
                  pl.BlockSpec((hb, cb), lambda i, j: (jnp.maximum(i * (tc // hb) - 1, 0), j)),
                  pl.BlockSpec((hb, cb), lambda i, j: (jnp.minimum((i + 1) * (tc // hb), nblk16 - 1), j)),
                  pl.BlockSpec((DN_CONV, cb), lambda i, j: (0, j)),
                  pl.BlockSpec((tc, LANE), lambda i, j: (i, 0)),
                  pl.BlockSpec((1, LANE), lambda i, j: (0, 0)),
                  pl.BlockSpec((1, LANE), lambda i, j: (0, 0))],
        out_specs=[pl.BlockSpec((tc, cb), lambda i, j: (i, j)),
                   pl.BlockSpec((tc, LANE), lambda i, j: (i, 0))],
        out_shape=[jax.ShapeDtypeStruct((r, 3 * dn_width), BF16),
                   jax.ShapeDtypeStruct((r, LANE), F32)],
        compiler_params=_cparams(("parallel", "arbitrary"), 32),
        name="dn_prep",
    )(p_main, p_main, p_main, conv_w, pba, alog_row, dtb_row)


def _split3(a):
    a1 = a.astype(BF16)
    r1 = a - a1.astype(F32)
    a2 = r1.astype(BF16)
    a3 = (r1 - a2.astype(F32)).astype(BF16)
    return a1, a2, a3


def _dot(a, b):
    return jnp.dot(a, b, preferred_element_type=F32)


def _bdot(a, b):
    return jnp.einsum("hik,hkj->hij", a, b, preferred_element_type=F32)


def _dn_local_kernel(q_ref, k_ref, v_ref, kt_ref, bg_ref, gr_ref,
                     u_ref, w_ref, qg_ref, qk_ref, kdt_ref, eg_ref, *, nh):
    cs = DN_CHUNK
    hd = DN_HEAD_DIM
    row = lax.broadcasted_iota(jnp.int32, (cs, cs), 0)
    col = lax.broadcasted_iota(jnp.int32, (cs, cs), 1)
    bg = bg_ref[...]
    gr = gr_ref[...]
    gtot = jnp.sum(bg, axis=0, keepdims=True)
    heads = [slice(h * hd, (h + 1) * hd) for h in range(nh)]
    qs = jnp.stack([q_ref[:, hs] for hs in heads])
    ks = jnp.stack([k_ref[:, hs] for hs in heads])
    vs = jnp.stack([v_ref[:, hs] for hs in heads])
    kts = jnp.stack([kt_ref[hs, :] for hs in heads])
    kq = _bdot(jnp.concatenate([ks, qs], axis=1), kts)
    kk = kq[:, :cs]
    qk_raw = kq[:, cs:]
    qf = qs.astype(F32)
    kf = ks.astype(F32)
    vf = vs.astype(F32)
    ktf = kts.astype(F32)
    for d in range(2):
        rel = (row - col) if d == 0 else (col - row)
        incl = (rel >= 0)[None]
        strict = (rel > 0)[None]
        m_incl = jnp.where(rel >= 0, 1.0, 0.0).astype(BF16)
        m_incl_t = jnp.where(rel <= 0, 1.0, 0.0).astype(BF16)
        eye = jnp.where(rel == 0, 1.0, 0.0)[None]
        gc_c = sum(_dot(m_incl, part) for part in _split3(bg))
        gc_r = sum(_dot(part, m_incl_t) for part in _split3(gr[d * nh:(d + 1) * nh]))
        lane_b = d * nh
        lane_g = 2 * nh + d * nh
        gcc = jnp.stack([gc_c[:, lane_g + h:lane_g + h + 1] for h in range(nh)])
        bc = jnp.stack([bg[:, lane_b + h:lane_b + h + 1] for h in range(nh)])
        gcr = jnp.stack([gc_r[h:h + 1, :] for h in range(nh)])
        gt = jnp.stack([gtot[:, lane_g + h:lane_g + h + 1] for h in range(nh)])
        decay = jnp.where(incl, jnp.exp(jnp.where(incl, gcc - gcr, 0.0)), 0.0)
        lmat = jnp.where(strict, bc * kk * decay, 0.0)
        egc = jnp.exp(gcc)
        rhs = jnp.concatenate([vf * bc, kf * (bc * egc)], axis=2).astype(BF16)
        lp = lmat.astype(BF16)
        t = eye - lmat
        n_sq = 1
        while 2 * n_sq < cs:
            lp = _bdot(lp, lp).astype(BF16)
            t = t + _bdot(t.astype(BF16), lp)
            n_sq *= 2
        uw = _bdot(t.astype(BF16), rhs)
        qk_ref[d] = (qk_raw * decay).astype(qk_ref.dtype)
        qg = (qf * egc).astype(qg_ref.dtype)
        kdt = (ktf * jnp.exp(gt - gcr)).astype(kdt_ref.dtype)
        eg_ref[d] = jnp.broadcast_to(jnp.exp(gt), (nh, 1, LANE))
        for h, hs in enumerate(heads):
            u_ref[d, :, hs] = uw[h, :, :hd]
            w_ref[d, :, hs] = uw[h, :, hd:].astype(w_ref.dtype)
            qg_ref[d, :, hs] = qg[h]
            kdt_ref[d, hs, :] = kdt[h]


def _dn_local(qkv, kt, bg, grow, n_heads):
    r = qkv.shape[0]
    nc = r // DN_CHUNK
    width = n_heads * DN_HEAD_DIM
    kern = functools.partial(_dn_local_kernel, nh=n_heads)
    return pl.pallas_call(
        kern,
        grid=(nc,),
        in_specs=[pl.BlockSpec((DN_CHUNK, width), lambda c: (c, 0)),
                  pl.BlockSpec((DN_CHUNK, width), lambda c: (c, 1)),
                  pl.BlockSpec((DN_CHUNK, width), lambda c: (c, 2)),
                  pl.BlockSpec((None, width, DN_CHUNK), lambda c: (c, 0, 0)),
                  pl.BlockSpec((DN_CHUNK, LANE), lambda c: (c, 0)),
                  pl.BlockSpec((None, grow.shape[1], DN_CHUNK), lambda c: (c, 0, 0))],
        out_specs=[pl.BlockSpec((2, DN_CHUNK, width), lambda c: (0, c, 0)),
                   pl.BlockSpec((2, DN_CHUNK, width), lambda c: (0, c, 0)),
                   pl.BlockSpec((2, DN_CHUNK, width), lambda c: (0, c, 0)),
                   pl.BlockSpec((2, None, n_heads, DN_CHUNK, DN_CHUNK), lambda c: (0, c, 0, 0, 0)),
                   pl.BlockSpec((2, None, width, DN_CHUNK), lambda c: (0, c, 0, 0)),
                   pl.BlockSpec((2, None, n_heads, 1, LANE), lambda c: (0, c, 0, 0, 0))],
        out_shape=[jax.ShapeDtypeStruct((2, r, width), F32),
                   jax.ShapeDtypeStruct((2, r, width), BF16),
                   jax.ShapeDtypeStruct((2, r, width), BF16),
                   jax.ShapeDtypeStruct((2, nc, n_heads, DN_CHUNK, DN_CHUNK), BF16),
                   jax.ShapeDtypeStruct((2, nc, width, DN_CHUNK), BF16),
                   jax.ShapeDtypeStruct((2, nc, n_heads, 1, LANE), F32)],
        compiler_params=_cparams(("parallel",), 40),
        name="dn_local",
    )(qkv, qkv, qkv, kt, bg, grow)


def _dn_seq_kernel(*refs, nh):
    ins, (of_ref, ob_ref, s_ref) = refs[:12], refs[12:]
    cs = DN_CHUNK
    hd = DN_HEAD_DIM

    @pl.when(pl.program_id(1) == 0)
    def _():
        s_ref[...] = jnp.zeros_like(s_ref)

    heads = [slice(h * hd, (h + 1) * hd) for h in range(nh)]
    for d, o_ref in enumerate((of_ref, ob_ref)):
        u_ref, w_ref, qg_ref, qk_ref, kdt_ref, eg_ref = ins[6 * d:6 * d + 6]
        lhs = jnp.stack([jnp.concatenate([w_ref[:, hs], qg_ref[:, hs]], axis=0) for hs in heads])
        s_old = s_ref[d]
        ws = _bdot(lhs, s_old.astype(BF16))
        u = jnp.stack([u_ref[:, hs] for hs in heads])
        v_new = (u - ws[:, :cs]).astype(BF16)
        lhs2 = jnp.concatenate([qk_ref[...], jnp.stack([kdt_ref[hs, :] for hs in heads])], axis=1)
        r2 = _bdot(lhs2, v_new)
        o = ws[:, cs:] + r2[:, :cs]
        for h, hs in enumerate(heads):
            o_ref[:, hs] = o[h]
        s_ref[d] = s_old * eg_ref[...] + r2[:, cs:]


def _dn_seq(u, w, qg, qk, kdt, eg, n_batch, ctx_len, seq, n_heads):
    r = u.shape[1]
    ncx = ctx_len // DN_CHUNK
    nlt = seq // DN_CHUNK
    width = n_heads * DN_HEAD_DIM

    def chunk(d):
        def f(b, s):
            pos_ctx = s if d == 0 else ncx - 1 - s
            pos_lat = s - ncx if d == 0 else nlt - 1 - (s - ncx)
            return jnp.where(s < ncx, b * ncx + pos_ctx, n_batch * ncx + b * nlt + pos_lat)
        return f

    in_specs, args = [], []
    for d in range(2):
        ch = chunk(d)
        in_specs += [
            pl.BlockSpec((None, DN_CHUNK, width), lambda b, s, ch=ch, d=d: (d, ch(b, s), 0)),
            pl.BlockSpec((None, DN_CHUNK, width), lambda b, s, ch=ch, d=d: (d, ch(b, s), 0)),
            pl.BlockSpec((None, DN_CHUNK, width), lambda b, s, ch=ch, d=d: (d, ch(b, s), 0)),
            pl.BlockSpec((None, None, n_heads, DN_CHUNK, DN_CHUNK),
                         lambda b, s, ch=ch, d=d: (d, ch(b, s), 0, 0, 0)),
            pl.BlockSpec((None, None, width, DN_CHUNK), lambda b, s, ch=ch, d=d: (d, ch(b, s), 0, 0)),
            pl.BlockSpec((None, None, n_heads, 1, LANE), lambda b, s, ch=ch, d=d: (d, ch(b, s), 0, 0, 0)),
        ]
        args += [u, w, qg, qk, kdt, eg]
    kern = functools.partial(_dn_seq_kernel, nh=n_heads)
    return pl.pallas_call(
        kern,
        grid=(n_batch, ncx + nlt),
        in_specs=in_specs,
        out_specs=[pl.BlockSpec((DN_CHUNK, width), lambda b, s, ch=chunk(0): (ch(b, s), 0)),
                   pl.BlockSpec((DN_CHUNK, width), lambda b, s, ch=chunk(1): (ch(b, s), 0))],
        out_shape=[jax.ShapeDtypeStruct((r, width), F32), jax.ShapeDtypeStruct((r, width), F32)],
        scratch_shapes=[pltpu.VMEM((2, n_heads, DN_HEAD_DIM, DN_HEAD_DIM), F32)],
        compiler_params=_cparams(("parallel", "arbitrary"), 40),
        name="dn_seq",
    )(*args)


def _sg_kernel(pu_ref, pv_ref, lnw_ref, lnb_ref, sgw_ref, sgbt_ref, y_ref, *, n_groups):
    tr = pu_ref.shape[0]
    gd = pu_ref.shape[1] // n_groups
    v = jax.nn.gelu(pv_ref[...].astype(F32))
    mu = jnp.mean(v, axis=-1, keepdims=True)
    xc = v - mu
    var = jnp.mean(xc * xc, axis=-1, keepdims=True)
    vv = (xc * lax.rsqrt(var + EPS) * lnw_ref[...] + lnb_ref[...]).astype(BF16)
    sgbt = sgbt_ref[...]
    for ch in range(tr // SG_CHUNK):
        rs = slice(ch * SG_CHUNK, (ch + 1) * SG_CHUNK)
        for g in range(n_groups):
            gs = slice(g * gd, (g + 1) * gd)
            mixed = _dot(sgw_ref[g], vv[rs, gs]) + sgbt[:, g:g + 1]
            u = jax.nn.gelu(pu_ref[rs, gs].astype(F32))
            y_ref[rs, gs] = (u * mixed).astype(y_ref.dtype)


def _sg(p_main, lnw, lnb, sgw, sgbt, layer, tr, u_blk, v_blk):
    r = p_main.shape[0]
    n_groups = sgw.shape[1]
    width = lnw.shape[1]
    kern = functools.partial(_sg_kernel, n_groups=n_groups)
    return pl.pallas_call(
        kern,
        grid=(r // tr,),
        in_specs=[pl.BlockSpec((tr, width), lambda i: (i, u_blk)),
                  pl.BlockSpec((tr, width), lambda i: (i, v_blk)),
                  pl.BlockSpec((1, width), lambda i: (0, 0)),
                  pl.BlockSpec((1, width), lambda i: (0, 0)),
                  pl.BlockSpec((None,) + sgw.shape[1:], lambda i: (layer, 0, 0, 0)),
                  pl.BlockSpec(sgbt.shape, lambda i: (0, 0))],
        out_specs=pl.BlockSpec((tr, width), lambda i: (i, 0)),
        out_shape=jax.ShapeDtypeStruct((r, width), BF16),
        compiler_params=_cparams(("parallel",), 32),
        name="spatial_gating",
    )(p_main, p_main, lnw, lnb, sgw, sgbt)


def _merge_kernel(of_ref, ob_ref, z_ref, dnw_ref, ysg_ref, wdn_ref, wsg_ref, gdn_ref, gsg_ref,
                  m_ref, ydn_ref):
    @pl.when(pl.program_id(1) == 0)
    def _():
        nw = dnw_ref[...]
        for h in range(of_ref.shape[1] // DN_HEAD_DIM):
            hs = slice(h * DN_HEAD_DIM, (h + 1) * DN_HEAD_DIM)
            o = of_ref[:, hs] + ob_ref[:, hs]
            ms = jnp.mean(o * o, axis=-1, keepdims=True)
            z = z_ref[:, hs].astype(F32)
            ydn_ref[:, hs] = (o * lax.rsqrt(ms + EPS) * nw * (z * jax.nn.sigmoid(z))).astype(BF16)

    a = _dot(ydn_ref[...], wdn_ref[...])
    b = _dot(ysg_ref[...], wsg_ref[...])
    m = (jax.nn.sigmoid(gdn_ref[...].astype(F32)) * a + jax.nn.sigmoid(gsg_ref[...].astype(F32)) * b)
    m_ref[...] = m.astype(m_ref.dtype)


def _merge(o_f, o_b, p_main, dnw, ysg, wdn, wsg, layer, tm, z_blk, gate_col0):
    r, width = ysg.shape
    d = wdn.shape[2]
    tn = min(512, d)
    g0 = gate_col0 // tn
    return pl.pallas_call(
        _merge_kernel,
        grid=(r // tm, d // tn),
        in_specs=[pl.BlockSpec((tm, width), lambda i, j: (i, 0)),
                  pl.BlockSpec((tm, width), lambda i, j: (i, 0)),
                  pl.BlockSpec((tm, width), lambda i, j: (i, z_blk)),
                  pl.BlockSpec((1, DN_HEAD_DIM), lambda i, j: (0, 0)),
                  pl.BlockSpec((tm, width), lambda i, j: (i, 0)),
                  pl.BlockSpec((None, width, tn), lambda i, j: (layer, 0, j)),
                  pl.BlockSpec((None, width, tn), lambda i, j: (layer, 0, j)),
                  pl.BlockSpec((tm, tn), lambda i, j: (i, g0 + j)),
                  pl.BlockSpec((tm, tn), lambda i, j: (i, g0 + d // tn + j))],
        out_specs=pl.BlockSpec((tm, tn), lambda i, j: (i, j)),
        out_shape=jax.ShapeDtypeStruct((r, d), BF16),
        scratch_shapes=[pltpu.VMEM((tm, width), BF16)],
        compiler_params=_cparams(("parallel", "arbitrary"), 48),
        name="merge",
    )(o_f, o_b, p_main, dnw, ysg, wdn, wsg, p_main, p_main)


def _out_proj_kernel(m_ref, w_ref, x_ref, ga_ref, o_ref):
    o_ref[...] = x_ref[...] + ga_ref[...] * _dot(m_ref[...], w_ref[...])


def _out_proj(m, w, layer, x, mod, tm, n_ctx_rows, seq):
    r, d = x.shape
    tn = min(1024, d)
    return pl.pallas_call(
        _out_proj_kernel,
        grid=(r // tm, d // tn),
        in_specs=[pl.BlockSpec((tm, d), lambda i, j: (i, 0)),
                  pl.BlockSpec((None, d, tn), lambda i, j: (layer, 0, j)),
                  pl.BlockSpec((tm, tn), lambda i, j: (i, j)),
                  _mod_spec(2, tm, n_ctx_rows, seq, d, ncol=tn)],
        out_specs=pl.BlockSpec((tm, tn), lambda i, j: (i, j)),
        out_shape=jax.ShapeDtypeStruct((r, d), F32),
        input_output_aliases={2: 0},
        compiler_params=_cparams(("parallel", "arbitrary"), 48),
        name="out_proj",
    )(m, w, x, mod)


def _pack_halves(xb):
    n2 = xb.shape[1] // 2
    lo = lax.bitcast_convert_type(xb[:, :n2].astype(F32), jnp.uint32)
    hi = lax.bitcast_convert_type(xb[:, n2:].astype(F32), jnp.uint32)
    return (lo >> 16) | (hi & jnp.uint32(0xFFFF0000))


def _unpack_halves_f32(w):
    lo = lax.bitcast_convert_type(w << 16, F32)
    hi = lax.bitcast_convert_type(w & jnp.uint32(0xFFFF0000), F32)
    return jnp.concatenate([lo, hi], axis=1)


def _unpack_halves(w):
    return _unpack_halves_f32(w).astype(BF16)


def _nt_dot(a, b):
    return lax.dot_general(a, b, (((1,), (1,)), ((), ())), preferred_element_type=F32)


def _route_kernel(x_ref, sh_ref, sc_ref, nw_ref, wrt_ref, bias_ref, h_ref, hp_ref, g_ref, eid_ref, wk_ref):
    x = x_ref[...]
    ms = jnp.mean(x * x, axis=-1, keepdims=True)
    h = x * lax.rsqrt(ms + EPS) * nw_ref[...] * (1.0 + sc_ref[...]) + sh_ref[...]
    hb = h.astype(BF16)
    h_ref[...] = hb
    hp_ref[...] = _pack_halves(hb)
    logits = _nt_dot(wrt_ref[...].astype(BF16), hb)
    s = jax.nn.sigmoid(logits)
    sel = s + bias_ref[...]
    ne, tm = sel.shape
    gsz = ne // N_EXPERT_GROUPS
    sub = lax.broadcasted_iota(jnp.int32, (gsz, tm), 0)
    gs_rows = []
    for g in range(N_EXPERT_GROUPS):
        blk = sel[g * gsz:(g + 1) * gsz, :]
        m1 = jnp.max(blk, axis=0, keepdims=True)
        i1 = jnp.min(jnp.where(blk == m1, sub, gsz), axis=0, keepdims=True)
        m2 = jnp.max(jnp.where(sub == i1, -jnp.inf, blk), axis=0, keepdims=True)
        gs_rows.append(m1 + m2)
    masked_blocks = []
    for g in range(N_EXPERT_GROUPS):
        rank = jnp.zeros((1, tm), F32)
        for g2 in range(N_EXPERT_GROUPS):
            if g2 == g:
                continue
            ahead = (gs_rows[g2] > gs_rows[g]) if g2 > g else (gs_rows[g2] >= gs_rows[g])
            rank = rank + jnp.where(ahead, 1.0, 0.0)
        keep = rank < TOPK_GROUPS
        masked_blocks.append(jnp.where(keep, sel[g * gsz:(g + 1) * gsz, :], -jnp.inf))
    masked = jnp.concatenate(masked_blocks, axis=0)
    eidx = lax.broadcasted_iota(jnp.int32, (ne, tm), 0)
    rank = jnp.zeros((ne, tm), F32)
    for e2 in range(ne):
        r2 = masked[e2:e2 + 1, :]
        tie = jnp.where(eidx > e2, 1.0, 0.0)
        rank = rank + jnp.where(r2 > masked, 1.0, jnp.where(r2 == masked, tie, 0.0))
    wts = jnp.where(rank < TOP_K, s, 0.0)
    gates = wts / jnp.sum(wts, axis=0, keepdims=True) * ROUTED_SCALE
    g_ref[...] = gates
    ids, wks = [], []
    for k in range(eid_ref.shape[0]):
        hit = rank == k
        ids.append(jnp.sum(jnp.where(hit, eidx, 0), axis=0, keepdims=True))
        wks.append(jnp.sum(jnp.where(hit, gates, 0.0), axis=0, keepdims=True))
    eid_ref[...] = jnp.concatenate(ids, axis=0)
    wk_ref[...] = jnp.concatenate(wks, axis=0)


def _route(x, mod, nw, wrt, bias, tm, n_ctx_rows, seq):
    r, d = x.shape
    ne = wrt.shape[0]
    return pl.pallas_call(
        _route_kernel,
        grid=(r // tm,),
        in_specs=[pl.BlockSpec((tm, d), lambda i: (i, 0)),
                  _mod_spec(3, tm, n_ctx_rows, seq, d),
                  _mod_spec(4, tm, n_ctx_rows, seq, d),
                  pl.BlockSpec((1, d), lambda i: (0, 0)),
                  pl.BlockSpec((ne, d), lambda i: (0, 0)),
                  pl.BlockSpec((ne, 1), lambda i: (0, 0))],
        out_specs=[pl.BlockSpec((tm, d), lambda i: (i, 0)),
                   pl.BlockSpec((tm, d // 2), lambda i: (i, 0)),
                   pl.BlockSpec((ne, tm), lambda i: (0, i)),
                   pl.BlockSpec((8, tm), lambda i: (0, i)),
                   pl.BlockSpec((8, tm), lambda i: (0, i))],
        out_shape=[jax.ShapeDtypeStruct((r, d), BF16), jax.ShapeDtypeStruct((r, d // 2), jnp.uint32),
                   jax.ShapeDtypeStruct((ne, r), F32), jax.ShapeDtypeStruct((8, r), jnp.int32),
                   jax.ShapeDtypeStruct((8, r), F32)],
        compiler_params=_cparams(("parallel",), 48),
        name="route",
    )(x, mod, mod, nw, wrt, bias)


def _shared_ffn_kernel(h_ref, wg_ref, wu_ref, wd_ref, y_ref):
    @pl.when(pl.program_id(1) == 0)
    def _():
        y_ref[...] = jnp.zeros_like(y_ref)

    h = h_ref[...]
    a = _dot(h, wg_ref[...])
    u = _dot(h, wu_ref[...])
    y_ref[...] += _dot((a * jax.nn.sigmoid(a) * u).astype(BF16), wd_ref[...])


def _shared_ffn(h, wg, wu, wd, layer, ff, tm):
    r, d = h.shape
    return pl.pallas_call(
        _shared_ffn_kernel,
        grid=(r // tm, wg.shape[2] // ff),
        in_specs=[pl.BlockSpec((tm, d), lambda i, e: (i, 0)),
                  pl.BlockSpec((None, d, ff), lambda i, e: (layer, 0, e)),
                  pl.BlockSpec((None, d, ff), lambda i, e: (layer, 0, e)),
                  pl.BlockSpec((None, ff, d), lambda i, e: (layer, e, 0))],
        out_specs=pl.BlockSpec((tm, d), lambda i, e: (i, 0)),
        out_shape=jax.ShapeDtypeStruct((r, d), F32),
        compiler_params=_cparams(("parallel", "arbitrary"), 48),
        name="shared_ffn",
    )(h, wg, wu, wd)


def _slot_wait(buf, sem, slot):
    pltpu.make_async_copy(buf.at[slot], buf.at[slot], sem.at[slot]).wait()


def _moe_group_kernel(tok_ref, texp_ref, h_hbm, wg_ref, wu_ref, wd_ref, gate_ref, y_ref, buf, sem,
                      *, tmx, nt):
    del texp_ref
    i = pl.program_id(0)

    def issue(tile, slot):
        def body(rb, carry):
            for sub in range(ISSUE_UNROLL):
                r = rb * ISSUE_UNROLL + sub
                t = tok_ref[tile * tmx + r]
                pltpu.make_async_copy(h_hbm.at[pl.ds(t, 1)], buf.at[slot, pl.ds(r, 1)], sem.at[slot]).start()
            return carry
        lax.fori_loop(0, tmx // ISSUE_UNROLL, body, 0)

    @pl.when(i == 0)
    def _():
        issue(0, 0)

    @pl.when(i + 1 < nt)
    def _():
        issue(i + 1, (i + 1) % 2)

    slot = i % 2
    _slot_wait(buf, sem, slot)
    x = _unpack_halves(buf[slot])
    a = _dot(x, wg_ref[...])
    u = _dot(x, wu_ref[...])
    hid = (a * jax.nn.sigmoid(a) * u * gate_ref[...]).astype(BF16)
    y_ref[...] = _pack_halves(_dot(hid, wd_ref[...]).astype(BF16))


def _moe_group(tok_pad, tile_exp, hp, wg, wu, wd, layer, gate_pad, tmx):
    p = tok_pad.shape[0]
    nt = p // tmx
    d = 2 * hp.shape[1]
    ff = wg.shape[3]
    kern = functools.partial(_moe_group_kernel, tmx=tmx, nt=nt)
    grid_spec = pltpu.PrefetchScalarGridSpec(
        num_scalar_prefetch=2,
        grid=(nt,),
        in_specs=[pl.BlockSpec(memory_space=pl.ANY),
                  pl.BlockSpec((None, None, d, ff), lambda i, tok, te: (layer, te[i], 0, 0)),
                  pl.BlockSpec((None, None, d, ff), lambda i, tok, te: (layer, te[i], 0, 0)),
                  pl.BlockSpec((None, None, ff, d), lambda i, tok, te: (layer, te[i], 0, 0)),
                  pl.BlockSpec((tmx, 1), lambda i, tok, te: (i, 0))],
        out_specs=pl.BlockSpec((tmx, d // 2), lambda i, tok, te: (i, 0)),
        scratch_shapes=[pltpu.VMEM((2, tmx, d // 2), jnp.uint32), pltpu.SemaphoreType.DMA((2,))],
    )
    return pl.pallas_call(
        kern,
        grid_spec=grid_spec,
        out_shape=jax.ShapeDtypeStruct((p, d // 2), jnp.uint32),
        compiler_params=_cparams(("arbitrary",), 48),
        name="moe_group",
    )(tok_pad, tile_exp, hp, wg, wu, wd, gate_pad)


def _moe_combine_kernel(pos_ref, y_hbm, x_ref, ys_ref, ga_ref, *rest, tt, nk, nt, t0, final):
    if final:
        fw_ref, o_ref, buf, sem = rest
    else:
        o_ref, buf, sem = rest
    i = pl.program_id(0)

    def issue(tile, slot):
        def body(rb, carry):
            for sub in range(2):
                r = rb * 2 + sub
                for k in range(nk):
                    p = pos_ref[((tile + t0) * tt + r) * nk + k]
                    pltpu.make_async_copy(y_hbm.at[pl.ds(p, 1)], buf.at[slot, k, pl.ds(r, 1)],
                                          sem.at[slot]).start(priority=k % DMA_THREADS)
            return carry
        lax.fori_loop(0, tt // 2, body, 0)

    @pl.when(i == 0)
    def _():
        issue(0, 0)

    @pl.when(i + 1 < nt)
    def _():
        issue(i + 1, (i + 1) % 2)

    slot = i % 2
    _slot_wait(buf, sem, slot)
    routed = _unpack_halves_f32(buf[slot, 0])
    for k in range(1, nk):
        routed = routed + _unpack_halves_f32(buf[slot, k])
    x = x_ref[...] + ga_ref[...] * (ys_ref[...] + routed)
    if final:
        ms = jnp.mean(x * x, axis=-1, keepdims=True)
        x = x * lax.rsqrt(ms + EPS) * fw_ref[...]
    o_ref[...] = x


def _moe_combine(pos_flat, y_sorted, x, ys, mod, tt, n_ctx_rows, seq, final_w=None):
    r, d = x.shape
    nk = pos_flat.shape[0] // r
    final = final_w is not None
    t0 = n_ctx_rows // tt if final else 0
    nt = r // tt - t0
    row_spec = pl.BlockSpec((tt, d), lambda i, pos: (i + t0, 0))
    in_specs = [pl.BlockSpec(memory_space=pl.ANY), row_spec, row_spec,
                _mod_spec(5, tt, n_ctx_rows, seq, d, tile0=t0)]
    args = [pos_flat, y_sorted, x, ys, mod]
    if final:
        in_specs.append(pl.BlockSpec((1, d), lambda i, pos: (0, 0)))
        args.append(final_w)
    kern = functools.partial(_moe_combine_kernel, tt=tt, nk=nk, nt=nt, t0=t0, final=final)
    grid_spec = pltpu.PrefetchScalarGridSpec(
        num_scalar_prefetch=1,
        grid=(nt,),
        in_specs=in_specs,
        out_specs=pl.BlockSpec((tt, d), lambda i, pos: (i, 0)),
        scratch_shapes=[pltpu.VMEM((2, nk, tt, d // 2), jnp.uint32), pltpu.SemaphoreType.DMA((2,))],
    )
    return pl.pallas_call(
        kern,
        grid_spec=grid_spec,
        out_shape=jax.ShapeDtypeStruct((nt * tt, d), F32),
        compiler_params=_cparams(("arbitrary",), 48),
        name="moe_combine",
    )(*args)


def _moe_plan(gates_t, eid_t, wk_t, tmx, p_rows):
    ne, r = gates_t.shape
    eid = eid_t[:TOP_K].T
    wk = wk_t[:TOP_K].T
    sel = (gates_t > 0).astype(jnp.int32)
    pos_in_e = jnp.cumsum(sel, axis=1) - 1
    cnt = pos_in_e[:, -1] + 1
    padded = ((cnt + tmx - 1) // tmx) * tmx
    ends = jnp.cumsum(padded)
    gstart = ends - padded
    cstart = jnp.cumsum(cnt) - cnt
    pos = gstart[eid] + jnp.take_along_axis(pos_in_e.T, eid, axis=1)
    pos_flat = pos.reshape(-1)
    tok_flat = jnp.repeat(jnp.arange(r, dtype=jnp.int32), TOP_K)
    _, tok_s, gate_s = lax.sort((pos_flat, tok_flat, wk.reshape(-1)), num_keys=1)
    nt = p_rows // tmx
    tile_exp = jnp.minimum(jnp.sum(ends[None, :] <= (jnp.arange(nt) * tmx)[:, None], axis=1), ne - 1)
    tile_exp = tile_exp.astype(jnp.int32)
    j = (jnp.arange(nt, dtype=jnp.int32) * tmx - gstart[tile_exp])[:, None] + jnp.arange(tmx, dtype=jnp.int32)[None, :]
    valid = j < cnt[tile_exp][:, None]
    ci = jnp.clip(cstart[tile_exp][:, None] + j, 0, r * TOP_K - 1).reshape(-1)
    valid = valid.reshape(-1)
    tok_pad = jnp.where(valid, tok_s[ci], 0).astype(jnp.int32)
    gate_pad = jnp.where(valid, gate_s[ci], 0.0)[:, None]
    return tok_pad, gate_pad, tile_exp, pos_flat.astype(jnp.int32)


def kernel(x, c, ctx, c_ctx, w_ada, b_ada, norm1_w, w_in, dn_conv_w, dn_a_log, dn_dt_bias, dn_norm_w, sg_ln_w, sg_ln_b, sg_w, sg_b, w_br_dn, w_br_sg, w_out, norm2_w, w_router, router_bias, w_exp_gate, w_exp_up, w_exp_down, w_sh_gate, w_sh_up, w_sh_down, final_norm_w):
    nb, seq, d = x.shape
    ctx_len = ctx.shape[1]
    depth = w_ada.shape[0]
    n_heads = dn_a_log.shape[-1]
    dn_width = dn_conv_w.shape[-1] // 3
    sg_width = sg_ln_w.shape[-1]
    ne, _, ff = w_exp_gate.shape[1:]
    sh_ff = w_sh_gate.shape[-1]
    n_ctx_rows = nb * ctx_len
    r = n_ctx_rows + nb * seq
    assert nb + 1 <= 8 and dn_width == n_heads * DN_HEAD_DIM and sh_ff % ff == 0
    tm = min(512, n_ctx_rows)
    tc = min(256, ctx_len)
    assert n_ctx_rows % tm == 0 and seq % tm == 0 and ctx_len % tc == 0 and seq % tc == 0
    tmx = 256
    tt = 64
    p_rows = -(-(r * TOP_K) // tmx) * tmx + ne * tmx
    segs =[(b * ctx_len, ctx_len) for b in range(nb)] + [(n_ctx_rows + b * seq, seq) for b in range(nb)]

    n_ba = 4 * n_heads
    c_qkv = 3 * dn_width
    z_col = c_qkv
    u_col = z_col + dn_width
    v_col = u_col + sg_width
    gate_col = v_col + sg_width
    assert dn_width == sg_width and z_col % dn_width == 0 and gate_col % 512 == 0

    cvec = jnp.concatenate([c_ctx[None, :], c, jnp.zeros((8 - 1 - nb, d), F32)], axis=0)
    mod_all = _ada(cvec, w_ada, b_ada)
    xa = jnp.concatenate([ctx.reshape(n_ctx_rows, d), x.reshape(nb * seq, d)], axis=0)

    w_main = jnp.concatenate([w_in[:, :, :c_qkv], w_in[:, :, c_qkv + n_ba:]], axis=2).astype(BF16)
    w_ba = jnp.pad(w_in[:, :, c_qkv:c_qkv + n_ba], ((0, 0), (0, 0), (0, LANE - n_ba))).astype(BF16)
    sgw_b, wdn_b, wsg_b, wout_b = (a.astype(BF16) for a in (sg_w, w_br_dn, w_br_sg, w_out))
    weg_b, weu_b, wed_b = (a.astype(BF16) for a in (w_exp_gate, w_exp_up, w_exp_down))
    wsg_sh_b, wsu_sh_b, wsd_sh_b = (a.astype(BF16) for a in (w_sh_gate, w_sh_up, w_sh_down))
    pad_ba = ((0, 0), (2 * n_heads, LANE - 4 * n_heads))

    out = None
    for l in range(depth):
        last = l == depth - 1
        mod = mod_all[l, :nb + 1].reshape(nb + 1, N_MOD, 1, d)

        p_main, pba = _in_proj(xa, mod, norm1_w[l][None, :], w_main, w_ba, l, tm, n_ctx_rows, seq)
        qkv, bg = _dn_prep(p_main, pba, dn_conv_w[l], jnp.pad(dn_a_log[l].reshape(1, -1), pad_ba),
                           jnp.pad(dn_dt_bias[l].reshape(1, -1), pad_ba), tc, segs, dn_width, n_heads)
        grow = bg[:, 2 * n_heads:4 * n_heads].reshape(r // DN_CHUNK, DN_CHUNK, 2 * n_heads).transpose(0, 2, 1)
        grow = jnp.pad(grow, ((0, 0), (0, max(0, BF16_SUBLANES - 2 * n_heads)), (0, 0)))
        kt = qkv[:, dn_width:2 * dn_width].reshape(r // DN_CHUNK, DN_CHUNK, dn_width).transpose(0, 2, 1)
        o_f, o_b = _dn_seq(*_dn_local(qkv, kt, bg, grow, n_heads), nb, ctx_len, seq, n_heads)
        ysg = _sg(p_main, sg_ln_w[l][None, :], sg_ln_b[l][None, :], sgw_b, sg_b[l].T, l,
                  min(256, ctx_len), u_col // sg_width, v_col // sg_width)
        m = _merge(o_f, o_b, p_main, dn_norm_w[l][None, :], ysg, wdn_b, wsg_b, l, tm, z_col // dn_width, gate_col)
        xa = _out_proj(m, wout_b, l, xa, mod, tm, n_ctx_rows, seq)

        h2, h2p, gates_t, eid_t, wk_t = _route(xa, mod, norm2_w[l][None, :], w_router[l].T,
                                               router_bias[l][:, None], min(256, tm), n_ctx_rows, seq)
        tok_pad, gate_pad, tile_exp, pos_flat = _moe_plan(gates_t, eid_t, wk_t, tmx, p_rows)
        y_sorted = _moe_group(tok_pad, tile_exp, h2p, weg_b, weu_b, wed_b, l, gate_pad, tmx)
        y_shared = _shared_ffn(h2, wsg_sh_b, wsu_sh_b, wsd_sh_b, l, ff, tm)
        if last:
            out = _moe_combine(pos_flat, y_sorted, xa, y_shared, mod, tt, n_ctx_rows, seq, final_norm_w[None, :])
        else:
            xa = _moe_combine(pos_flat, y_sorted, xa, y_shared, mod, tt, n_ctx_rows, seq)
    return out.reshape(nb, seq, d)
```

```python
import functools

import jax
import jax.numpy as jnp
from jax import lax
from jax.experimental import pallas as pl
from jax.experimental.pallas import tpu as pltpu

F32 = jnp.float32
BF16 = jnp.bfloat16

EPS = 1e-6
N_MOD = 6
DN_HEAD_DIM = 128
DN_CONV = 5
DN_CHUNK = 64
SG_CHUNK = 128
TOP_K = 6
N_EXPERT_GROUPS = 8
TOPK_GROUPS = 4
ROUTED_SCALE = 2.5

LANE = 128
BF16_SUBLANES = 16
DMA_THREADS = 2
MIB = 1024 * 1024


def _cparams(sem, vmem_mib):
    return pltpu.CompilerParams(dimension_semantics=sem, vmem_limit_bytes=vmem_mib * MIB)


def _col_tile(n, pref):
    t = min(pref, n)
    while n % t or t % LANE:
        t -= LANE
    return t


def _mod_row(tile_start, n_ctx_rows, seq):
    return jnp.where(tile_start < n_ctx_rows, 0, 1 + (tile_start - n_ctx_rows) // seq)


def _mod_spec(which, tm, n_ctx_rows, seq, d, ncol=None, tile0=0):
    if ncol is None:
        return pl.BlockSpec((None, None, 1, d),
                            lambda i, *_: (_mod_row((i + tile0) * tm, n_ctx_rows, seq), which, 0, 0))
    return pl.BlockSpec((None, None, 1, ncol),
                        lambda i, j: (_mod_row(i * tm, n_ctx_rows, seq), which, 0, j))


def _ada_kernel(c_ref, w_ref, b_ref, o_ref):
    c = c_ref[...]
    a = (c * jax.nn.sigmoid(c)).astype(BF16)
    o_ref[...] = jnp.dot(a, w_ref[...].astype(BF16), preferred_element_type=F32) + b_ref[...]


def _ada(cvec, w_ada, b_ada):
    nl, d, n = w_ada.shape
    tn = min(512, n)
    return pl.pallas_call(
        _ada_kernel,
        grid=(nl, n // tn),
        in_specs=[pl.BlockSpec((8, d), lambda l, j: (0, 0)),
                  pl.BlockSpec((None, d, tn), lambda l, j: (l, 0, j)),
                  pl.BlockSpec((None, 1, tn), lambda l, j: (l, 0, j))],
        out_specs=pl.BlockSpec((None, 8, tn), lambda l, j: (l, 0, j)),
        out_shape=jax.ShapeDtypeStruct((nl, 8, n), F32),
        compiler_params=_cparams(("parallel", "parallel"), 40),
        name="ada",
    )(cvec, w_ada, b_ada.reshape(nl, 1, n))


def _in_proj_kernel(x_ref, sh_ref, sc_ref, nw_ref, w_ref, wba_ref, p_ref, pba_ref, h_ref):
    @pl.when(pl.program_id(1) == 0)
    def _():
        x = x_ref[...]
        ms = jnp.mean(x * x, axis=-1, keepdims=True)
        y = x * lax.rsqrt(ms + EPS) * nw_ref[...]
        hb = (y * (1.0 + sc_ref[...]) + sh_ref[...]).astype(BF16)
        h_ref[...] = hb
        pba_ref[...] = jnp.dot(hb, wba_ref[...], preferred_element_type=F32)

    p_ref[...] = jnp.dot(h_ref[...], w_ref[...], preferred_element_type=F32).astype(p_ref.dtype)


def _in_proj(x, mod, nw, w_main, w_ba, layer, tm, n_ctx_rows, seq):
    r, d = x.shape
    n = w_main.shape[2]
    tn = _col_tile(n, 1024)
    return pl.pallas_call(
        _in_proj_kernel,
        grid=(r // tm, n // tn),
        in_specs=[pl.BlockSpec((tm, d), lambda i, j: (i, 0)),
                  _mod_spec(0, tm, n_ctx_rows, seq, d),
                  _mod_spec(1, tm, n_ctx_rows, seq, d),
                  pl.BlockSpec((1, d), lambda i, j: (0, 0)),
                  pl.BlockSpec((None, d, tn), lambda i, j: (layer, 0, j)),
                  pl.BlockSpec((None, d, LANE), lambda i, j: (layer, 0, 0))],
        out_specs=[pl.BlockSpec((tm, tn), lambda i, j: (i, j)),
                   pl.BlockSpec((tm, LANE), lambda i, j: (i, 0))],
        out_shape=[jax.ShapeDtypeStruct((r, n), BF16), jax.ShapeDtypeStruct((r, LANE), F32)],
        scratch_shapes=[pltpu.VMEM((tm, d), BF16)],
        compiler_params=_cparams(("parallel", "arbitrary"), 48),
        name="in_proj",
    )(x, mod, mod, nw, w_main, w_ba)


def _dn_prep_kernel(cur_ref, prev_ref, next_ref, cw_ref, pba_ref, alog_ref, dtb_ref,
                    qkv_ref, bg_ref, *, tc, seg_starts, seg_ends, dn_width, n_heads):
    i = pl.program_id(0)
    j = pl.program_id(1)
    cb = cur_ref.shape[1]
    start = i * tc
    is_start = functools.reduce(jnp.logical_or, [start == s for s in seg_starts])
    is_end = functools.reduce(jnp.logical_or, [start + tc == s for s in seg_ends])
    x = cur_ref[...].astype(F32)
    hp = prev_ref[...].astype(F32)
    hn = next_ref[...].astype(F32)
    nh = hp.shape[0]
    keep_p = jnp.where(is_start, 0.0, 1.0)
    keep_n = jnp.where(is_end, 0.0, 1.0)
    pm2 = hp[nh - 2:nh - 1] * keep_p
    pm1 = hp[nh - 1:nh] * keep_p
    np1 = hn[0:1] * keep_n
    np2 = hn[1:2] * keep_n
    row = lax.broadcasted_iota(jnp.int32, (tc, cb), 0)
    xm1 = jnp.where(row == 0, pm1, pltpu.roll(x, 1, axis=0))
    xm2 = jnp.where(row == 0, pm2, jnp.where(row == 1, pm1, pltpu.roll(x, 2, axis=0)))
    xp1 = jnp.where(row == tc - 1, np1, pltpu.roll(x, tc - 1, axis=0))
    xp2 = jnp.where(row == tc - 1, np2, jnp.where(row == tc - 2, np1, pltpu.roll(x, tc - 2, axis=0)))
    cw = cw_ref[...]
    y = cw[0:1] * xm2 + cw[1:2] * xm1 + cw[2:3] * x + cw[3:4] * xp1 + cw[4:5] * xp2
    y = y * jax.nn.sigmoid(y)
    kind = (j * cb) // dn_width
    qscale = jnp.where(kind == 0, DN_HEAD_DIM ** -0.5, 1.0)
    for hh in range(cb // DN_HEAD_DIM):
        seg = y[:, hh * DN_HEAD_DIM:(hh + 1) * DN_HEAD_DIM]
        ss = jnp.sum(seg * seg, axis=-1, keepdims=True)
        fac = jnp.where(kind == 2, 1.0, lax.rsqrt(ss + EPS) * qscale)
        qkv_ref[:, hh * DN_HEAD_DIM:(hh + 1) * DN_HEAD_DIM] = (seg * fac).astype(qkv_ref.dtype)

    @pl.when(j == 0)
    def _():
        p = pba_ref[...]
        lane = lax.broadcasted_iota(jnp.int32, p.shape, 1)
        beta = jax.nn.sigmoid(p)
        z = p + dtb_ref[...]
        sp = jnp.maximum(z, 0.0) + jnp.log(1.0 + jnp.exp(-jnp.abs(z)))
        g = -jnp.exp(alog_ref[...]) * sp
        bg_ref[...] = jnp.where(lane < 2 * n_heads, beta, g)


def _dn_prep(p_main, pba, conv_w, alog_row, dtb_row, tc, segs, dn_width, n_heads):
    r = p_main.shape[0]
    cb = min(1024, dn_width)
    hb = BF16_SUBLANES
    nblk16 = r // hb
    seg_starts = tuple(s for s, _ in segs)
    seg_ends = tuple(s + n for s, n in segs)
    kern = functools.partial(_dn_prep_kernel, tc=tc, seg_starts=seg_starts, seg_ends=seg_ends,
                             dn_width=dn_width, n_heads=n_heads)
    return pl.pallas_call(
        kern,
        grid=(r // tc, 3 * dn_width // cb),
        in_specs=[pl.BlockSpec((tc, cb), lambda i, j: (i, j)),
                  pl.BlockSpec((hb, cb), lambda i, j: (jnp.maximum(i * (tc // hb) - 1, 0), j)),
                  pl.BlockSpec((hb, cb), lambda i, j: (jnp.minimum((i + 1) * (tc // hb), nblk16 - 1), j)),
                  pl.BlockSpec((DN_CONV, cb), lambda i, j: (0, j)),
                  pl.BlockSpec((tc, LANE), lambda i, j: (i, 0)),
                  pl.BlockSpec((1, LANE), lambda i, j: (0, 0)),
                  pl.BlockSpec((1, LANE), lambda i, j: (0, 0))],
        out_specs=[pl.BlockSpec((tc, cb), lambda i, j: (i, j)),
                   pl.BlockSpec((tc, LANE), lambda i, j: (i, 0))],
        out_shape=[jax.ShapeDtypeStruct((r, 3 * dn_width), BF16),
                   jax.ShapeDtypeStruct((r, LANE), F32)],
        compiler_params=_cparams(("parallel", "arbitrary"), 32),
        name="dn_prep",
    )(p_main, p_main, p_main, conv_w, pba, alog_row, dtb_row)


def _split3(a):
    a1 = a.astype(BF16)
    r1 = a - a1.astype(F32)
    a2 = r1.astype(BF16)
    a3 = (r1 - a2.astype(F32)).astype(BF16)
    return a1, a2, a3


def _dot(a, b):
    return jnp.dot(a, b, preferred_element_type=F32)


def _bdot(a, b):
    return jnp.einsum("hik,hkj->hij", a, b, preferred_element_type=F32)


def _dn_local_kernel(q_ref, k_ref, v_ref, kt_ref, bg_ref, gr_ref,
                     u_ref, w_ref, qg_ref, qk_ref, kdt_ref, eg_ref, *, nh):
    cs = DN_CHUNK
    hd = DN_HEAD_DIM
    row = lax.broadcasted_iota(jnp.int32, (cs, cs), 0)
    col = lax.broadcasted_iota(jnp.int32, (cs, cs), 1)
    bg = bg_ref[...]
    gr = gr_ref[...]
    gtot = jnp.sum(bg, axis=0, keepdims=True)
    heads = [slice(h * hd, (h + 1) * hd) for h in range(nh)]
    qs = jnp.stack([q_ref[:, hs] for hs in heads])
    ks = jnp.stack([k_ref[:, hs] for hs in heads])
    vs = jnp.stack([v_ref[:, hs] for hs in heads])
    kts = jnp.stack([kt_ref[hs, :] for hs in heads])
    kq = _bdot(jnp.concatenate([ks, qs], axis=1), kts)
    kk = kq[:, :cs]
    qk_raw = kq[:, cs:]
    qf = qs.astype(F32)
    kf = ks.astype(F32)
    vf = vs.astype(F32)
    ktf = kts.astype(F32)
    for d in range(2):
        rel = (row - col) if d == 0 else (col - row)
        incl = (rel >= 0)[None]
        strict = (rel > 0)[None]
        m_incl = jnp.where(rel >= 0, 1.0, 0.0).astype(BF16)
        m_incl_t = jnp.where(rel <= 0, 1.0, 0.0).astype(BF16)
        eye = jnp.where(rel == 0, 1.0, 0.0)[None]
        gc_c = sum(_dot(m_incl, part) for part in _split3(bg))
        gc_r = sum(_dot(part, m_incl_t) for part in _split3(gr[d * nh:(d + 1) * nh]))
        lane_b = d * nh
        lane_g = 2 * nh + d * nh
        gcc = jnp.stack([gc_c[:, lane_g + h:lane_g + h + 1] for h in range(nh)])
        bc = jnp.stack([bg[:, lane_b + h:lane_b + h + 1] for h in range(nh)])
        gcr = jnp.stack([gc_r[h:h + 1, :] for h in range(nh)])
        gt = jnp.stack([gtot[:, lane_g + h:lane_g + h + 1] for h in range(nh)])
        decay = jnp.where(incl, jnp.exp(jnp.where(incl, gcc - gcr, 0.0)), 0.0)
        lmat = jnp.where(strict, bc * kk * decay, 0.0)
        egc = jnp.exp(gcc)
        rhs = jnp.concatenate([vf * bc, kf * (bc * egc)], axis=2).astype(BF16)
        lp = lmat.astype(BF16)
        t = eye - lmat
        n_sq = 1
        while 2 * n_sq < cs:
            lp = _bdot(lp, lp).astype(BF16)
            t = t + _bdot(t.astype(BF16), lp)
            n_sq *= 2
        uw = _bdot(t.astype(BF16), rhs)
        qk_ref[d] = (qk_raw * decay).astype(qk_ref.dtype)
        qg = (qf * egc).astype(qg_ref.dtype)
        kdt = (ktf * jnp.exp(gt - gcr)).astype(kdt_ref.dtype)
        eg_ref[d] = jnp.broadcast_to(jnp.exp(gt), (nh, 1, LANE))
        for h, hs in enumerate(heads):
            u_ref[d, :, hs] = uw[h, :, :hd]
            w_ref[d, :, hs] = uw[h, :, hd:].astype(w_ref.dtype)
            qg_ref[d, :, hs] = qg[h]
            kdt_ref[d, hs, :] = kdt[h]


def _dn_local(qkv, kt, bg, grow, n_heads):
    r = qkv.shape[0]
    nc = r // DN_CHUNK
    width = n_heads * DN_HEAD_DIM
    kern = functools.partial(_dn_local_kernel, nh=n_heads)
    return pl.pallas_call(
        kern,
        grid=(nc,),
        in_specs=[pl.BlockSpec((DN_CHUNK, width), lambda c: (c, 0)),
                  pl.BlockSpec((DN_CHUNK, width), lambda c: (c, 1)),
                  pl.BlockSpec((DN_CHUNK, width), lambda c: (c, 2)),
                  pl.BlockSpec((None, width, DN_CHUNK), lambda c: (c, 0, 0)),
                  pl.BlockSpec((DN_CHUNK, LANE), lambda c: (c, 0)),
                  pl.BlockSpec((None, grow.shape[1], DN_CHUNK), lambda c: (c, 0, 0))],
        out_specs=[pl.BlockSpec((2, DN_CHUNK, width), lambda c: (0, c, 0)),
                   pl.BlockSpec((2, DN_CHUNK, width), lambda c: (0, c, 0)),
                   pl.BlockSpec((2, DN_CHUNK, width), lambda c: (0, c, 0)),
                   pl.BlockSpec((2, None, n_heads, DN_CHUNK, DN_CHUNK), lambda c: (0, c, 0, 0, 0)),
                   pl.BlockSpec((2, None, width, DN_CHUNK), lambda c: (0, c, 0, 0)),
                   pl.BlockSpec((2, None, n_heads, 1, LANE), lambda c: (0, c, 0, 0, 0))],
        out_shape=[jax.ShapeDtypeStruct((2, r, width), F32),
                   jax.ShapeDtypeStruct((2, r, width), BF16),
                   jax.ShapeDtypeStruct((2, r, width), BF16),
                   jax.ShapeDtypeStruct((2, nc, n_heads, DN_CHUNK, DN_CHUNK), BF16),
                   jax.ShapeDtypeStruct((2, nc, width, DN_CHUNK), BF16),
                   jax.ShapeDtypeStruct((2, nc, n_heads, 1, LANE), F32)],
        compiler_params=_cparams(("parallel",), 40),
        name="dn_local",
    )(qkv, qkv, qkv, kt, bg, grow)


def _dn_seq_kernel(*refs, nh):
    ins, (of_ref, ob_ref, s_ref) = refs[:12], refs[12:]
    cs = DN_CHUNK
    hd = DN_HEAD_DIM

    @pl.when(pl.program_id(1) == 0)
    def _():
        s_ref[...] = jnp.zeros_like(s_ref)

    heads = [slice(h * hd, (h + 1) * hd) for h in range(nh)]
    for d, o_ref in enumerate((of_ref, ob_ref)):
        u_ref, w_ref, qg_ref, qk_ref, kdt_ref, eg_ref = ins[6 * d:6 * d + 6]
        lhs = jnp.stack([jnp.concatenate([w_ref[:, hs], qg_ref[:, hs]], axis=0) for hs in heads])
        s_old = s_ref[d]
        ws = _bdot(lhs, s_old.astype(BF16))
        u = jnp.stack([u_ref[:, hs] for hs in heads])
        v_new = (u - ws[:, :cs]).astype(BF16)
        lhs2 = jnp.concatenate([qk_ref[...], jnp.stack([kdt_ref[hs, :] for hs in heads])], axis=1)
        r2 = _bdot(lhs2, v_new)
        o = ws[:, cs:] + r2[:, :cs]
        for h, hs in enumerate(heads):
            o_ref[:, hs] = o[h]
        s_ref[d] = s_old * eg_ref[...] + r2[:, cs:]


def _dn_seq(u, w, qg, qk, kdt, eg, n_batch, ctx_len, seq, n_heads):
    r = u.shape[1]
    ncx = ctx_len // DN_CHUNK
    nlt = seq // DN_CHUNK
    width = n_heads * DN_HEAD_DIM

    def chunk(d):
        def f(b, s):
            pos_ctx = s if d == 0 else ncx - 1 - s
            pos_lat = s - ncx if d == 0 else nlt - 1 - (s - ncx)
            return jnp.where(s < ncx, b * ncx + pos_ctx, n_batch * ncx + b * nlt + pos_lat)
        return f

    in_specs, args = [], []
    for d in range(2):
        ch = chunk(d)
        in_specs += [
            pl.BlockSpec((None, DN_CHUNK, width), lambda b, s, ch=ch, d=d: (d, ch(b, s), 0)),
            pl.BlockSpec((None, DN_CHUNK, width), lambda b, s, ch=ch, d=d: (d, ch(b, s), 0)),
            pl.BlockSpec((None, DN_CHUNK, width), lambda b, s, ch=ch, d=d: (d, ch(b, s), 0)),
            pl.BlockSpec((None, None, n_heads, DN_CHUNK, DN_CHUNK),
                         lambda b, s, ch=ch, d=d: (d, ch(b, s), 0, 0, 0)),
            pl.BlockSpec((None, None, width, DN_CHUNK), lambda b, s, ch=ch, d=d: (d, ch(b, s), 0, 0)),
            pl.BlockSpec((None, None, n_heads, 1, LANE), lambda b, s, ch=ch, d=d: (d, ch(b, s), 0, 0, 0)),
        ]
        args += [u, w, qg, qk, kdt, eg]
    kern = functools.partial(_dn_seq_kernel, nh=n_heads)
    return pl.pallas_call(
        kern,
        grid=(n_batch, ncx + nlt),
        in_specs=in_specs,
        out_specs=[pl.BlockSpec((DN_CHUNK, width), lambda b, s, ch=chunk(0): (ch(b, s), 0)),
                   pl.BlockSpec((DN_CHUNK, width), lambda b, s, ch=chunk(1): (ch(b, s), 0))],
        out_shape=[jax.ShapeDtypeStruct((r, width), F32), jax.ShapeDtypeStruct((r, width), F32)],
        scratch_shapes=[pltpu.VMEM((2, n_heads, DN_HEAD_DIM, DN_HEAD_DIM), F32)],
        compiler_params=_cparams(("parallel", "arbitrary"), 40),
        name="dn_seq",
    )(*args)


def _sg_kernel(pu_ref, pv_ref, lnw_ref, lnb_ref, sgw_ref, sgbt_ref, y_ref, *, n_groups):
    tr = pu_ref.shape[0]
    gd = pu_ref.shape[1] // n_groups
    v = jax.nn.gelu(pv_ref[...].astype(F32))
    mu = jnp.mean(v, axis=-1, keepdims=True)
    xc = v - mu
    var = jnp.mean(xc * xc, axis=-1, keepdims=True)
    vv = (xc * lax.rsqrt(var + EPS) * lnw_ref[...] + lnb_ref[...]).astype(BF16)
    sgbt = sgbt_ref[...]
    for ch in range(tr // SG_CHUNK):
        rs = slice(ch * SG_CHUNK, (ch + 1) * SG_CHUNK)
        for g in range(n_groups):
            gs = slice(g * gd, (g + 1) * gd)
            mixed = _dot(sgw_ref[g], vv[rs, gs]) + sgbt[:, g:g + 1]
            u = jax.nn.gelu(pu_ref[rs, gs].astype(F32))
            y_ref[rs, gs] = (u * mixed).astype(y_ref.dtype)


def _sg(p_main, lnw, lnb, sgw, sgbt, layer, tr, u_blk, v_blk):
    r = p_main.shape[0]
    n_groups = sgw.shape[1]
    width = lnw.shape[1]
    kern = functools.partial(_sg_kernel, n_groups=n_groups)
    return pl.pallas_call(
        kern,
        grid=(r // tr,),
        in_specs=[pl.BlockSpec((tr, width), lambda i: (i, u_blk)),
                  pl.BlockSpec((tr, width), lambda i: (i, v_blk)),
                  pl.BlockSpec((1, width), lambda i: (0, 0)),
                  pl.BlockSpec((1, width), lambda i: (0, 0)),
                  pl.BlockSpec((None,) + sgw.shape[1:], lambda i: (layer, 0, 0, 0)),
                  pl.BlockSpec(sgbt.shape, lambda i: (0, 0))],
        out_specs=pl.BlockSpec((tr, width), lambda i: (i, 0)),
        out_shape=jax.ShapeDtypeStruct((r, width), BF16),
        compiler_params=_cparams(("parallel",), 32),
        name="spatial_gating",
    )(p_main, p_main, lnw, lnb, sgw, sgbt)


def _merge_kernel(of_ref, ob_ref, z_ref, dnw_ref, ysg_ref, wdn_ref, wsg_ref, gdn_ref, gsg_ref,
                  m_ref, ydn_ref):
    @pl.when(pl.program_id(1) == 0)
    def _():
        nw = dnw_ref[...]
        for h in range(of_ref.shape[1] // DN_HEAD_DIM):
            hs = slice(h * DN_HEAD_DIM, (h + 1) * DN_HEAD_DIM)
            o = of_ref[:, hs] + ob_ref[:, hs]
            ms = jnp.mean(o * o, axis=-1, keepdims=True)
            z = z_ref[:, hs].astype(F32)
            ydn_ref[:, hs] = (o * lax.rsqrt(ms + EPS) * nw * (z * jax.nn.sigmoid(z))).astype(BF16)

    a = _dot(ydn_ref[...], wdn_ref[...])
    b = _dot(ysg_ref[...], wsg_ref[...])
    m = (jax.nn.sigmoid(gdn_ref[...].astype(F32)) * a + jax.nn.sigmoid(gsg_ref[...].astype(F32)) * b)
    m_ref[...] = m.astype(m_ref.dtype)


def _merge(o_f, o_b, p_main, dnw, ysg, wdn, wsg, layer, tm, z_blk, gate_col0):
    r, width = ysg.shape
    d = wdn.shape[2]
    tn = min(512, d)
    g0 = gate_col0 // tn
    return pl.pallas_call(
        _merge_kernel,
        grid=(r // tm, d // tn),
        in_specs=[pl.BlockSpec((tm, width), lambda i, j: (i, 0)),
                  pl.BlockSpec((tm, width), lambda i, j: (i, 0)),
                  pl.BlockSpec((tm, width), lambda i, j: (i, z_blk)),
                  pl.BlockSpec((1, DN_HEAD_DIM), lambda i, j: (0, 0)),
                  pl.BlockSpec((tm, width), lambda i, j: (i, 0)),
                  pl.BlockSpec((None, width, tn), lambda i, j: (layer, 0, j)),
                  pl.BlockSpec((None, width, tn), lambda i, j: (layer, 0, j)),
                  pl.BlockSpec((tm, tn), lambda i, j: (i, g0 + j)),
                  pl.BlockSpec((tm, tn), lambda i, j: (i, g0 + d // tn + j))],
        out_specs=pl.BlockSpec((tm, tn), lambda i, j: (i, j)),
        out_shape=jax.ShapeDtypeStruct((r, d), BF16),
        scratch_shapes=[pltpu.VMEM((tm, width), BF16)],
        compiler_params=_cparams(("parallel", "arbitrary"), 48),
        name="merge",
    )(o_f, o_b, p_main, dnw, ysg, wdn, wsg, p_main, p_main)


def _out_proj_kernel(m_ref, w_ref, x_ref, ga_ref, o_ref):
    o_ref[...] = x_ref[...] + ga_ref[...] * _dot(m_ref[...], w_ref[...])


def _out_proj(m, w, layer, x, mod, tm, n_ctx_rows, seq):
    r, d = x.shape
    tn = min(1024, d)
    return pl.pallas_call(
        _out_proj_kernel,
        grid=(r // tm, d // tn),
        in_specs=[pl.BlockSpec((tm, d), lambda i, j: (i, 0)),
                  pl.BlockSpec((None, d, tn), lambda i, j: (layer, 0, j)),
                  pl.BlockSpec((tm, tn), lambda i, j: (i, j)),
                  _mod_spec(2, tm, n_ctx_rows, seq, d, ncol=tn)],
        out_specs=pl.BlockSpec((tm, tn), lambda i, j: (i, j)),
        out_shape=jax.ShapeDtypeStruct((r, d), F32),
        input_output_aliases={2: 0},
        compiler_params=_cparams(("parallel", "arbitrary"), 48),
        name="out_proj",
    )(m, w, x, mod)


def _pack_halves(xb):
    n2 = xb.shape[1] // 2
    lo = lax.bitcast_convert_type(xb[:, :n2].astype(F32), jnp.uint32)
    hi = lax.bitcast_convert_type(xb[:, n2:].astype(F32), jnp.uint32)
    return (lo >> 16) | (hi & jnp.uint32(0xFFFF0000))


def _unpack_halves_f32(w):
    lo = lax.bitcast_convert_type(w << 16, F32)
    hi = lax.bitcast_convert_type(w & jnp.uint32(0xFFFF0000), F32)
    return jnp.concatenate([lo, hi], axis=1)


def _unpack_halves(w):
    return _unpack_halves_f32(w).astype(BF16)


def _nt_dot(a, b):
    return lax.dot_general(a, b, (((1,), (1,)), ((), ())), preferred_element_type=F32)


def _route_kernel(x_ref, sh_ref, sc_ref, nw_ref, wrt_ref, bias_ref, h_ref, hp_ref, g_ref, eid_ref, wk_ref):
    x = x_ref[...]
    ms = jnp.mean(x * x, axis=-1, keepdims=True)
    h = x * lax.rsqrt(ms + EPS) * nw_ref[...] * (1.0 + sc_ref[...]) + sh_ref[...]
    hb = h.astype(BF16)
    h_ref[...] = hb
    hp_ref[...] = _pack_halves(hb)
    logits = _nt_dot(wrt_ref[...].astype(BF16), hb)
    s = jax.nn.sigmoid(logits)
    sel = s + bias_ref[...]
    ne, tm = sel.shape
    gsz = ne // N_EXPERT_GROUPS
    sub = lax.broadcasted_iota(jnp.int32, (gsz, tm), 0)
    gs_rows = []
    for g in range(N_EXPERT_GROUPS):
        blk = sel[g * gsz:(g + 1) * gsz, :]
        m1 = jnp.max(blk, axis=0, keepdims=True)
        i1 = jnp.min(jnp.where(blk == m1, sub, gsz), axis=0, keepdims=True)
        m2 = jnp.max(jnp.where(sub == i1, -jnp.inf, blk), axis=0, keepdims=True)
        gs_rows.append(m1 + m2)
    masked_blocks = []
    for g in range(N_EXPERT_GROUPS):
        rank = jnp.zeros((1, tm), F32)
        for g2 in range(N_EXPERT_GROUPS):
            if g2 == g:
                continue
            ahead = (gs_rows[g2] > gs_rows[g]) if g2 > g else (gs_rows[g2] >= gs_rows[g])
            rank = rank + jnp.where(ahead, 1.0, 0.0)
        keep = rank < TOPK_GROUPS
        masked_blocks.append(jnp.where(keep, sel[g * gsz:(g + 1) * gsz, :], -jnp.inf))
    masked = jnp.concatenate(masked_blocks, axis=0)
    eidx = lax.broadcasted_iota(jnp.int32, (ne, tm), 0)
    rank = jnp.zeros((ne, tm), F32)
    for e2 in range(ne):
        r2 = masked[e2:e2 + 1, :]
        tie = jnp.where(eidx > e2, 1.0, 0.0)
        rank = rank + jnp.where(r2 > masked, 1.0, jnp.where(r2 == masked, tie, 0.0))
    wts = jnp.where(rank < TOP_K, s, 0.0)
    gates = wts / jnp.sum(wts, axis=0, keepdims=True) * ROUTED_SCALE
    g_ref[...] = gates
    ids, wks = [], []
    for k in range(eid_ref.shape[0]):
        hit = rank == k
        ids.append(jnp.sum(jnp.where(hit, eidx, 0), axis=0, keepdims=True))
        wks.append(jnp.sum(jnp.where(hit, gates, 0.0), axis=0, keepdims=True))
    eid_ref[...] = jnp.concatenate(ids, axis=0)
    wk_ref[...] = jnp.concatenate(wks, axis=0)


def _route(x, mod, nw, wrt, bias, tm, n_ctx_rows, seq):
    r, d = x.shape
    ne = wrt.shape[0]
    return pl.pallas_call(
        _route_kernel,
        grid=(r // tm,),
        in_specs=[pl.BlockSpec((tm, d), lambda i: (i, 0)),
                  _mod_spec(3, tm, n_ctx_rows, seq, d),
                  _mod_spec(4, tm, n_ctx_rows, seq, d),
                  pl.BlockSpec((1, d), lambda i: (0, 0)),
                  pl.BlockSpec((ne, d), lambda i: (0, 0)),
                  pl.BlockSpec((ne, 1), lambda i: (0, 0))],
        out_specs=[pl.BlockSpec((tm, d), lambda i: (i, 0)),
                   pl.BlockSpec((tm, d // 2), lambda i: (i, 0)),
                   pl.BlockSpec((ne, tm), lambda i: (0, i)),
                   pl.BlockSpec((8, tm), lambda i: (0, i)),
                   pl.BlockSpec((8, tm), lambda i: (0, i))],
        out_shape=[jax.ShapeDtypeStruct((r, d), BF16), jax.ShapeDtypeStruct((r, d // 2), jnp.uint32),
                   jax.ShapeDtypeStruct((ne, r), F32), jax.ShapeDtypeStruct((8, r), jnp.int32),
                   jax.ShapeDtypeStruct((8, r), F32)],
        compiler_params=_cparams(("parallel",), 48),
        name="route",
    )(x, mod, mod, nw, wrt, bias)


def _shared_ffn_kernel(h_ref, wg_ref, wu_ref, wd_ref, y_ref):
    @pl.when(pl.program_id(1) == 0)
    def _():
        y_ref[...] = jnp.zeros_like(y_ref)

    h = h_ref[...]
    a = _dot(h, wg_ref[...])
    u = _dot(h, wu_ref[...])
    y_ref[...] += _dot((a * jax.nn.sigmoid(a) * u).astype(BF16), wd_ref[...])


def _shared_ffn(h, wg, wu, wd, layer, ff, tm):
    r, d = h.shape
    return pl.pallas_call(
        _shared_ffn_kernel,
        grid=(r // tm, wg.shape[2] // ff),
        in_specs=[pl.BlockSpec((tm, d), lambda i, e: (i, 0)),
                  pl.BlockSpec((None, d, ff), lambda i, e: (layer, 0, e)),
                  pl.BlockSpec((None, d, ff), lambda i, e: (layer, 0, e)),
                  pl.BlockSpec((None, ff, d), lambda i, e: (layer, e, 0))],
        out_specs=pl.BlockSpec((tm, d), lambda i, e: (i, 0)),
        out_shape=jax.ShapeDtypeStruct((r, d), F32),
        compiler_params=_cparams(("parallel", "arbitrary"), 48),
        name="shared_ffn",
    )(h, wg, wu, wd)


def _slot_wait(buf, sem, slot):
    pltpu.make_async_copy(buf.at[slot], buf.at[slot], sem.at[slot]).wait()


def _moe_group_kernel(tok_ref, texp_ref, h_hbm, wg_ref, wu_ref, wd_ref, gate_ref, y_ref, buf, sem,
                      *, tmx, nt):
    del texp_ref
    i = pl.program_id(0)

    def row_copy(tile, slot, r):
        t = tok_ref[tile * tmx + r]
        return pltpu.make_async_copy(h_hbm.at[pl.ds(t, 1)], buf.at[slot, pl.ds(r, 1)], sem.at[slot])

    @pl.when(i == 0)
    def _():
        def body(r, carry):
            row_copy(0, 0, r).start()
            return carry
        lax.fori_loop(0, tmx, body, 0)

    slot = i % 2
    _slot_wait(buf, sem, slot)
    nxt = jnp.minimum(i + 1, nt - 1)
    for r in range(tmx):
        row_copy(nxt, 1 - slot, r).start(priority=r % DMA_THREADS)
    x = _unpack_halves(buf[slot])
    a = _dot(x, wg_ref[...])
    u = _dot(x, wu_ref[...])
    hid = (a * jax.nn.sigmoid(a) * u * gate_ref[...]).astype(BF16)
    y_ref[...] = _pack_halves(_dot(hid, wd_ref[...]).astype(BF16))

    @pl.when(i == nt - 1)
    def _():
        _slot_wait(buf, sem, 1 - slot)


def _moe_group(tok_pad, tile_exp, hp, wg, wu, wd, layer, gate_pad, tmx):
    p = tok_pad.shape[0]
    nt = p // tmx
    d = 2 * hp.shape[1]
    ff = wg.shape[3]
    kern = functools.partial(_moe_group_kernel, tmx=tmx, nt=nt)
    grid_spec = pltpu.PrefetchScalarGridSpec(
        num_scalar_prefetch=2,
        grid=(nt,),
        in_specs=[pl.BlockSpec(memory_space=pl.ANY),
                  pl.BlockSpec((None, None, d, ff), lambda i, tok, te: (layer, te[i], 0, 0)),
                  pl.BlockSpec((None, None, d, ff), lambda i, tok, te: (layer, te[i], 0, 0)),
                  pl.BlockSpec((None, None, ff, d), lambda i, tok, te: (layer, te[i], 0, 0)),
                  pl.BlockSpec((tmx, 1), lambda i, tok, te: (i, 0))],
        out_specs=pl.BlockSpec((tmx, d // 2), lambda i, tok, te: (i, 0)),
        scratch_shapes=[pltpu.VMEM((2, tmx, d // 2), jnp.uint32), pltpu.SemaphoreType.DMA((2,))],
    )
    return pl.pallas_call(
        kern,
        grid_spec=grid_spec,
        out_shape=jax.ShapeDtypeStruct((p, d // 2), jnp.uint32),
        compiler_params=_cparams(("arbitrary",), 48),
        name="moe_group",
    )(tok_pad, tile_exp, hp, wg, wu, wd, gate_pad)


def _moe_combine_kernel(pos_ref, y_hbm, x_ref, ys_ref, ga_ref, *rest, tt, nk, nt, t0, final):
    if final:
        fw_ref, o_ref, buf, sem = rest
    else:
        o_ref, buf, sem = rest
    i = pl.program_id(0)

    def row_copy(tile, slot, r, k):
        p = pos_ref[((tile + t0) * tt + r) * nk + k]
        return pltpu.make_async_copy(y_hbm.at[pl.ds(p, 1)], buf.at[slot, k, pl.ds(r, 1)], sem.at[slot])

    @pl.when(i == 0)
    def _():
        def body(r, carry):
            for k in range(nk):
                row_copy(0, 0, r, k).start()
            return carry
        lax.fori_loop(0, tt, body, 0)

    slot = i % 2
    _slot_wait(buf, sem, slot)
    nxt = jnp.minimum(i + 1, nt - 1)
    for r in range(tt):
        for k in range(nk):
            row_copy(nxt, 1 - slot, r, k).start(priority=k % DMA_THREADS)
    routed = _unpack_halves_f32(buf[slot, 0])
    for k in range(1, nk):
        routed = routed + _unpack_halves_f32(buf[slot, k])
    x = x_ref[...] + ga_ref[...] * (ys_ref[...] + routed)
    if final:
        ms = jnp.mean(x * x, axis=-1, keepdims=True)
        x = x * lax.rsqrt(ms + EPS) * fw_ref[...]
    o_ref[...] = x

    @pl.when(i == nt - 1)
    def _():
        _slot_wait(buf, sem, 1 - slot)


def _moe_combine(pos_flat, y_sorted, x, ys, mod, tt, n_ctx_rows, seq, final_w=None):
    r, d = x.shape
    nk = pos_flat.shape[0] // r
    final = final_w is not None
    t0 = n_ctx_rows // tt if final else 0
    nt = r // tt - t0
    row_spec = pl.BlockSpec((tt, d), lambda i, pos: (i + t0, 0))
    in_specs = [pl.BlockSpec(memory_space=pl.ANY), row_spec, row_spec,
                _mod_spec(5, tt, n_ctx_rows, seq, d, tile0=t0)]
    args = [pos_flat, y_sorted, x, ys, mod]
    if final:
        in_specs.append(pl.BlockSpec((1, d), lambda i, pos: (0, 0)))
        args.append(final_w)
    kern = functools.partial(_moe_combine_kernel, tt=tt, nk=nk, nt=nt, t0=t0, final=final)
    grid_spec = pltpu.PrefetchScalarGridSpec(
        num_scalar_prefetch=1,
        grid=(nt,),
        in_specs=in_specs,
        out_specs=pl.BlockSpec((tt, d), lambda i, pos: (i, 0)),
        scratch_shapes=[pltpu.VMEM((2, nk, tt, d // 2), jnp.uint32), pltpu.SemaphoreType.DMA((2,))],
    )
    return pl.pallas_call(
        kern,
        grid_spec=grid_spec,
        out_shape=jax.ShapeDtypeStruct((nt * tt, d), F32),
        compiler_params=_cparams(("arbitrary",), 48),
        name="moe_combine",
    )(*args)


def _moe_plan(gates_t, eid_t, wk_t, tmx, p_rows):
    ne, r = gates_t.shape
    eid = eid_t[:TOP_K].T
    wk = wk_t[:TOP_K].T
    sel = (gates_t > 0).astype(jnp.int32)
    pos_in_e = jnp.cumsum(sel, axis=1) - 1
    cnt = pos_in_e[:, -1] + 1
    padded = ((cnt + tmx - 1) // tmx) * tmx
    ends = jnp.cumsum(padded)
    gstart = ends - padded
    cstart = jnp.cumsum(cnt) - cnt
    pos = gstart[eid] + jnp.take_along_axis(pos_in_e.T, eid, axis=1)
    pos_flat = pos.reshape(-1)
    tok_flat = jnp.repeat(jnp.arange(r, dtype=jnp.int32), TOP_K)
    _, tok_s, gate_s = lax.sort((pos_flat, tok_flat, wk.reshape(-1)), num_keys=1)
    nt = p_rows // tmx
    tile_exp = jnp.minimum(jnp.sum(ends[None, :] <= (jnp.arange(nt) * tmx)[:, None], axis=1), ne - 1)
    tile_exp = tile_exp.astype(jnp.int32)
    j = (jnp.arange(nt, dtype=jnp.int32) * tmx - gstart[tile_exp])[:, None] + jnp.arange(tmx, dtype=jnp.int32)[None, :]
    valid = j < cnt[tile_exp][:, None]
    ci = jnp.clip(cstart[tile_exp][:, None] + j, 0, r * TOP_K - 1).reshape(-1)
    valid = valid.reshape(-1)
    tok_pad = jnp.where(valid, tok_s[ci], 0).astype(jnp.int32)
    gate_pad = jnp.where(valid, gate_s[ci], 0.0)[:, None]
    return tok_pad, gate_pad, tile_exp, pos_flat.astype(jnp.int32)


def kernel(x, c, ctx, c_ctx, w_ada, b_ada, norm1_w, w_in, dn_conv_w, dn_a_log, dn_dt_bias, dn_norm_w, sg_ln_w, sg_ln_b, sg_w, sg_b, w_br_dn, w_br_sg, w_out, norm2_w, w_router, router_bias, w_exp_gate, w_exp_up, w_exp_down, w_sh_gate, w_sh_up, w_sh_down, final_norm_w):
    nb, seq, d = x.shape
    ctx_len = ctx.shape[1]
    depth = w_ada.shape[0]
    n_heads = dn_a_log.shape[-1]
    dn_width = dn_conv_w.shape[-1] // 3
    sg_width = sg_ln_w.shape[-1]
    ne, _, ff = w_exp_gate.shape[1:]
    sh_ff = w_sh_gate.shape[-1]
    n_ctx_rows = nb * ctx_len
    r = n_ctx_rows + nb * seq
    assert nb + 1 <= 8 and dn_width == n_heads * DN_HEAD_DIM and sh_ff % ff == 0
    tm = min(512, n_ctx_rows)
    tc = min(256, ctx_len)
    assert n_ctx_rows % tm == 0 and seq % tm == 0 and ctx_len % tc == 0 and seq % tc == 0
    tmx = 256
    tt = 64
    p_rows = -(-(r * TOP_K) // tmx) * tmx + ne * tmx
    segs =[(b * ctx_len, ctx_len) for b in range(nb)] + [(n_ctx_rows + b * seq, seq) for b in range(nb)]

    n_ba = 4 * n_heads
    c_qkv = 3 * dn_width
    z_col = c_qkv
    u_col = z_col + dn_width
    v_col = u_col + sg_width
    gate_col = v_col + sg_width
    assert dn_width == sg_width and z_col % dn_width == 0 and gate_col % 512 == 0

    cvec = jnp.concatenate([c_ctx[None, :], c, jnp.zeros((8 - 1 - nb, d), F32)], axis=0)
    mod_all = _ada(cvec, w_ada, b_ada)
    xa = jnp.concatenate([ctx.reshape(n_ctx_rows, d), x.reshape(nb * seq, d)], axis=0)

    w_main = jnp.concatenate([w_in[:, :, :c_qkv], w_in[:, :, c_qkv + n_ba:]], axis=2).astype(BF16)
    w_ba = jnp.pad(w_in[:, :, c_qkv:c_qkv + n_ba], ((0, 0), (0, 0), (0, LANE - n_ba))).astype(BF16)
    sgw_b, wdn_b, wsg_b, wout_b = (a.astype(BF16) for a in (sg_w, w_br_dn, w_br_sg, w_out))
    weg_b, weu_b, wed_b = (a.astype(BF16) for a in (w_exp_gate, w_exp_up, w_exp_down))
    wsg_sh_b, wsu_sh_b, wsd_sh_b = (a.astype(BF16) for a in (w_sh_gate, w_sh_up, w_sh_down))
    pad_ba = ((0, 0), (2 * n_heads, LANE - 4 * n_heads))

    out = None
    for l in range(depth):
        last = l == depth - 1
        mod = mod_all[l, :nb + 1].reshape(nb + 1, N_MOD, 1, d)

        p_main, pba = _in_proj(xa, mod, norm1_w[l][None, :], w_main, w_ba, l, tm, n_ctx_rows, seq)
        qkv, bg = _dn_prep(p_main, pba, dn_conv_w[l], jnp.pad(dn_a_log[l].reshape(1, -1), pad_ba),
                           jnp.pad(dn_dt_bias[l].reshape(1, -1), pad_ba), tc, segs, dn_width, n_heads)
        grow = bg[:, 2 * n_heads:4 * n_heads].reshape(r // DN_CHUNK, DN_CHUNK, 2 * n_heads).transpose(0, 2, 1)
        grow = jnp.pad(grow, ((0, 0), (0, max(0, BF16_SUBLANES - 2 * n_heads)), (0, 0)))
        kt = qkv[:, dn_width:2 * dn_width].reshape(r // DN_CHUNK, DN_CHUNK, dn_width).transpose(0, 2, 1)
        o_f, o_b = _dn_seq(*_dn_local(qkv, kt, bg, grow, n_heads), nb, ctx_len, seq, n_heads)
        ysg = _sg(p_main, sg_ln_w[l][None, :], sg_ln_b[l][None, :], sgw_b, sg_b[l].T, l,
                  min(256, ctx_len), u_col // sg_width, v_col // sg_width)
        m = _merge(o_f, o_b, p_main, dn_norm_w[l][None, :], ysg, wdn_b, wsg_b, l, tm, z_col // dn_width, gate_col)
        xa = _out_proj(m, wout_b, l, xa, mod, tm, n_ctx_rows, seq)

        h2, h2p, gates_t, eid_t, wk_t = _route(xa, mod, norm2_w[l][None, :], w_router[l].T,
                                               router_bias[l][:, None], min(256, tm), n_ctx_rows, seq)
        tok_pad, gate_pad, tile_exp, pos_flat = _moe_plan(gates_t, eid_t, wk_t, tmx, p_rows)
        y_sorted = _moe_group(tok_pad, tile_exp, h2p, weg_b, weu_b, wed_b, l, gate_pad, tmx)
        y_shared = _shared_ffn(h2, wsg_sh_b, wsu_sh_b, wsd_sh_b, l, ff, tm)
        if last:
            out = _moe_combine(pos_flat, y_sorted, xa, y_shared, mod, tt, n_ctx_rows, seq, final_norm_w[None, :])
        else:
            xa = _moe_combine(pos_flat, y_sorted, xa, y_shared, mod, tt, n_ctx_rows, seq)
    return out.reshape(nb, seq, d)
```

```python
import functools

import jax
import jax.numpy as jnp
from jax import lax
from jax.experimental import pallas as pl
from jax.experimental.pallas import tpu as pltpu

F32 = jnp.float32
BF16 = jnp.bfloat16

EPS = 1e-6
N_MOD = 6
DN_HEAD_DIM = 128
DN_CONV = 5
DN_CHUNK = 64
SG_CHUNK = 128
TOP_K = 6
N_EXPERT_GROUPS = 8
TOPK_GROUPS = 4
ROUTED_SCALE = 2.5

LANE = 128
SUBLANES = 8
BF16_SUBLANES = 16
DMA_THREADS = 2
MIB = 1024 * 1024


def _cparams(sem, vmem_mib):
    return pltpu.CompilerParams(dimension_semantics=sem, vmem_limit_bytes=vmem_mib * MIB)


def _col_tile(n, pref):
    t = min(pref, n)
    while n % t or t % LANE:
        t -= LANE
    return t


def _mod_row(tile_start, n_ctx_rows, seq):
    return jnp.where(tile_start < n_ctx_rows, 0, 1 + (tile_start - n_ctx_rows) // seq)


def _mod_spec(which, tm, n_ctx_rows, seq, d, ncol=None, tile0=0):
    if ncol is None:
        return pl.BlockSpec((None, None, 1, d),
                            lambda i, *_: (_mod_row((i + tile0) * tm, n_ctx_rows, seq), which, 0, 0))
    return pl.BlockSpec((None, None, 1, ncol),
                        lambda i, j: (_mod_row(i * tm, n_ctx_rows, seq), which, 0, j))


def _ada_kernel(c_ref, w_ref, b_ref, o_ref):
    c = c_ref[...]
    a = (c * jax.nn.sigmoid(c)).astype(BF16)
    o_ref[...] = jnp.dot(a, w_ref[...].astype(BF16), preferred_element_type=F32) + b_ref[...]


def _ada(cvec, w_ada, b_ada):
    nl, d, n = w_ada.shape
    tn = min(512, n)
    return pl.pallas_call(
        _ada_kernel,
        grid=(nl, n // tn),
        in_specs=[pl.BlockSpec((8, d), lambda l, j: (0, 0)),
                  pl.BlockSpec((None, d, tn), lambda l, j: (l, 0, j)),
                  pl.BlockSpec((None, 1, tn), lambda l, j: (l, 0, j))],
        out_specs=pl.BlockSpec((None, 8, tn), lambda l, j: (l, 0, j)),
        out_shape=jax.ShapeDtypeStruct((nl, 8, n), F32),
        compiler_params=_cparams(("parallel", "parallel"), 40),
        name="ada",
    )(cvec, w_ada, b_ada.reshape(nl, 1, n))


def _in_proj_kernel(x_ref, sh_ref, sc_ref, nw_ref, w_ref, wba_ref, p_ref, pba_ref, h_ref):
    @pl.when(pl.program_id(1) == 0)
    def _():
        x = x_ref[...]
        ms = jnp.mean(x * x, axis=-1, keepdims=True)
        y = x * lax.rsqrt(ms + EPS) * nw_ref[...]
        hb = (y * (1.0 + sc_ref[...]) + sh_ref[...]).astype(BF16)
        h_ref[...] = hb
        pba_ref[...] = jnp.dot(hb, wba_ref[...], preferred_element_type=F32)

    p_ref[...] = jnp.dot(h_ref[...], w_ref[...], preferred_element_type=F32).astype(p_ref.dtype)


def _in_proj(x, mod, nw, w_main, w_ba, layer, tm, n_ctx_rows, seq):
    r, d = x.shape
    n = w_main.shape[2]
    tn = _col_tile(n, 1024)
    return pl.pallas_call(
        _in_proj_kernel,
        grid=(r // tm, n // tn),
        in_specs=[pl.BlockSpec((tm, d), lambda i, j: (i, 0)),
                  _mod_spec(0, tm, n_ctx_rows, seq, d),
                  _mod_spec(1, tm, n_ctx_rows, seq, d),
                  pl.BlockSpec((1, d), lambda i, j: (0, 0)),
                  pl.BlockSpec((None, d, tn), lambda i, j: (layer, 0, j)),
                  pl.BlockSpec((None, d, LANE), lambda i, j: (layer, 0, 0))],
        out_specs=[pl.BlockSpec((tm, tn), lambda i, j: (i, j)),
                   pl.BlockSpec((tm, LANE), lambda i, j: (i, 0))],
        out_shape=[jax.ShapeDtypeStruct((r, n), BF16), jax.ShapeDtypeStruct((r, LANE), F32)],
        scratch_shapes=[pltpu.VMEM((tm, d), BF16)],
        compiler_params=_cparams(("parallel", "arbitrary"), 48),
        name="in_proj",
    )(x, mod, mod, nw, w_main, w_ba)


def _dn_prep_kernel(cur_ref, prev_ref, next_ref, cw_ref, pba_ref, alog_ref, dtb_ref,
                    qkv_ref, bg_ref, *, tc, seg_starts, seg_ends, dn_width, n_heads):
    i = pl.program_id(0)
    j = pl.program_id(1)
    cb = cur_ref.shape[1]
    start = i * tc
    is_start = functools.reduce(jnp.logical_or, [start == s for s in seg_starts])
    is_end = functools.reduce(jnp.logical_or, [start + tc == s for s in seg_ends])
    x = cur_ref[...].astype(F32)
    hp = prev_ref[...].astype(F32)
    hn = next_ref[...].astype(F32)
    nh = hp.shape[0]
    keep_p = jnp.where(is_start, 0.0, 1.0)
    keep_n = jnp.where(is_end, 0.0, 1.0)
    pm2 = hp[nh - 2:nh - 1] * keep_p
    pm1 = hp[nh - 1:nh] * keep_p
    np1 = hn[0:1] * keep_n
    np2 = hn[1:2] * keep_n
    row = lax.broadcasted_iota(jnp.int32, (tc, cb), 0)
    xm1 = jnp.where(row == 0, pm1, pltpu.roll(x, 1, axis=0))
    xm2 = jnp.where(row == 0, pm2, jnp.where(row == 1, pm1, pltpu.roll(x, 2, axis=0)))
    xp1 = jnp.where(row == tc - 1, np1, pltpu.roll(x, tc - 1, axis=0))
    xp2 = jnp.where(row == tc - 1, np2, jnp.where(row == tc - 2, np1, pltpu.roll(x, tc - 2, axis=0)))
    cw = cw_ref[...]
    y = cw[0:1] * xm2 + cw[1:2] * xm1 + cw[2:3] * x + cw[3:4] * xp1 + cw[4:5] * xp2
    y = y * jax.nn.sigmoid(y)
    kind = (j * cb) // dn_width
    qscale = jnp.where(kind == 0, DN_HEAD_DIM ** -0.5, 1.0)
    for hh in range(cb // DN_HEAD_DIM):
        seg = y[:, hh * DN_HEAD_DIM:(hh + 1) * DN_HEAD_DIM]
        ss = jnp.sum(seg * seg, axis=-1, keepdims=True)
        fac = jnp.where(kind == 2, 1.0, lax.rsqrt(ss + EPS) * qscale)
        qkv_ref[:, hh * DN_HEAD_DIM:(hh + 1) * DN_HEAD_DIM] = (seg * fac).astype(qkv_ref.dtype)

    @pl.when(j == 0)
    def _():
        p = pba_ref[...]
        lane = lax.broadcasted_iota(jnp.int32, p.shape, 1)
        beta = jax.nn.sigmoid(p)
        z = p + dtb_ref[...]
        sp = jnp.maximum(z, 0.0) + jnp.log(1.0 + jnp.exp(-jnp.abs(z)))
        g = -jnp.exp(alog_ref[...]) * sp
        bg_ref[...] = jnp.where(lane < 2 * n_heads, beta, g)


def _dn_prep(p_main, pba, conv_w, alog_row, dtb_row, tc, segs, dn_width, n_heads):
    r = p_main.shape[0]
    cb = min(1024, dn_width)
    hb = BF16_SUBLANES
    nblk16 = r // hb
    seg_starts = tuple(s for s, _ in segs)
    seg_ends = tuple(s + n for s, n in segs)
    kern = functools.partial(_dn_prep_kernel, tc=tc, seg_starts=seg_starts, seg_ends=seg_ends,
                             dn_width=dn_width, n_heads=n_heads)
    return pl.pallas_call(
        kern,
        grid=(r // tc, 3 * dn_width // cb),
        in_specs=[pl.BlockSpec((tc, cb), lambda i, j: (i, j)),
                  pl.BlockSpec((hb, cb), lambda i, j: (jnp.maximum(i * (tc // hb) - 1, 0), j)),
                  pl.BlockSpec((hb, cb), lambda i, j: (jnp.minimum((i + 1) * (tc // hb), nblk16 - 1), j)),
                  pl.BlockSpec((DN_CONV, cb), lambda i, j: (0, j)),
                  pl.BlockSpec((tc, LANE), lambda i, j: (i, 0)),
                  pl.BlockSpec((1, LANE), lambda i, j: (0, 0)),
                  pl.BlockSpec((1, LANE), lambda i, j: (0, 0))],
        out_specs=[pl.BlockSpec((tc, cb), lambda i, j: (i, j)),
                   pl.BlockSpec((tc, LANE), lambda i, j: (i, 0))],
        out_shape=[jax.ShapeDtypeStruct((r, 3 * dn_width), BF16),
                   jax.ShapeDtypeStruct((r, LANE), F32)],
        compiler_params=_cparams(("parallel", "arbitrary"), 32),
        name="dn_prep",
    )(p_main, p_main, p_main, conv_w, pba, alog_row, dtb_row)


def _split3(a):
    a1 = a.astype(BF16)
    r1 = a - a1.astype(F32)
    a2 = r1.astype(BF16)
    a3 = (r1 - a2.astype(F32)).astype(BF16)
    return a1, a2, a3


def _dot(a, b):
    return jnp.dot(a, b, preferred_element_type=F32)


def _bdot(a, b):
    return jnp.einsum("hik,hkj->hij", a, b, preferred_element_type=F32)


def _dn_local_kernel(q_ref, k_ref, v_ref, kt_ref, bg_ref, gr_ref,
                     u_ref, w_ref, qg_ref, qk_ref, kdt_ref, eg_ref, *, nh):
    cs = DN_CHUNK
    hd = DN_HEAD_DIM
    row = lax.broadcasted_iota(jnp.int32, (cs, cs), 0)
    col = lax.broadcasted_iota(jnp.int32, (cs, cs), 1)
    bg = bg_ref[...]
    gr = gr_ref[...]
    gtot = jnp.sum(bg, axis=0, keepdims=True)
    heads = [slice(h * hd, (h + 1) * hd) for h in range(nh)]
    qs = jnp.stack([q_ref[:, hs] for hs in heads])
    ks = jnp.stack([k_ref[:, hs] for hs in heads])
    vs = jnp.stack([v_ref[:, hs] for hs in heads])
    kts = jnp.stack([kt_ref[hs, :] for hs in heads])
    kq = _bdot(jnp.concatenate([ks, qs], axis=1), kts)
    kk = kq[:, :cs]
    qk_raw = kq[:, cs:]
    qf = qs.astype(F32)
    kf = ks.astype(F32)
    vf = vs.astype(F32)
    ktf = kts.astype(F32)
    for d in range(2):
        rel = (row - col) if d == 0 else (col - row)
        incl = (rel >= 0)[None]
        strict = (rel > 0)[None]
        m_incl = jnp.where(rel >= 0, 1.0, 0.0).astype(BF16)
        m_incl_t = jnp.where(rel <= 0, 1.0, 0.0).astype(BF16)
        eye = jnp.where(rel == 0, 1.0, 0.0)[None]
        gc_c = sum(_dot(m_incl, part) for part in _split3(bg))
        gc_r = sum(_dot(part, m_incl_t) for part in _split3(gr[d * nh:(d + 1) * nh]))
        lane_b = d * nh
        lane_g = 2 * nh + d * nh
        gcc = jnp.stack([gc_c[:, lane_g + h:lane_g + h + 1] for h in range(nh)])
        bc = jnp.stack([bg[:, lane_b + h:lane_b + h + 1] for h in range(nh)])
        gcr = jnp.stack([gc_r[h:h + 1, :] for h in range(nh)])
        gt = jnp.stack([gtot[:, lane_g + h:lane_g + h + 1] for h in range(nh)])
        decay = jnp.where(incl, jnp.exp(jnp.where(incl, gcc - gcr, 0.0)), 0.0)
        lmat = jnp.where(strict, bc * kk * decay, 0.0)
        egc = jnp.exp(gcc)
        rhs = jnp.concatenate([vf * bc, kf * (bc * egc)], axis=2).astype(BF16)
        lp = lmat.astype(BF16)
        t = eye - lmat
        n_sq = 1
        while 2 * n_sq < cs:
            lp = _bdot(lp, lp).astype(BF16)
            t = t + _bdot(t.astype(BF16), lp)
            n_sq *= 2
        uw = _bdot(t.astype(BF16), rhs)
        qk_ref[d] = (qk_raw * decay).astype(qk_ref.dtype)
        qg = (qf * egc).astype(qg_ref.dtype)
        kdt = (ktf * jnp.exp(gt - gcr)).astype(kdt_ref.dtype)
        eg_ref[d] = jnp.broadcast_to(jnp.exp(gt), (nh, 1, LANE))
        for h, hs in enumerate(heads):
            u_ref[d, :, hs] = uw[h, :, :hd]
            w_ref[d, :, hs] = uw[h, :, hd:].astype(w_ref.dtype)
            qg_ref[d, :, hs] = qg[h]
            kdt_ref[d, hs, :] = kdt[h]


def _dn_local(qkv, kt, bg, grow, n_heads):
    r = qkv.shape[0]
    nc = r // DN_CHUNK
    width = n_heads * DN_HEAD_DIM
    kern = functools.partial(_dn_local_kernel, nh=n_heads)
    return pl.pallas_call(
        kern,
        grid=(nc,),
        in_specs=[pl.BlockSpec((DN_CHUNK, width), lambda c: (c, 0)),
                  pl.BlockSpec((DN_CHUNK, width), lambda c: (c, 1)),
                  pl.BlockSpec((DN_CHUNK, width), lambda c: (c, 2)),
                  pl.BlockSpec((None, width, DN_CHUNK), lambda c: (c, 0, 0)),
                  pl.BlockSpec((DN_CHUNK, LANE), lambda c: (c, 0)),
                  pl.BlockSpec((None, grow.shape[1], DN_CHUNK), lambda c: (c, 0, 0))],
        out_specs=[pl.BlockSpec((2, DN_CHUNK, width), lambda c: (0, c, 0)),
                   pl.BlockSpec((2, DN_CHUNK, width), lambda c: (0, c, 0)),
                   pl.BlockSpec((2, DN_CHUNK, width), lambda c: (0, c, 0)),
                   pl.BlockSpec((2, None, n_heads, DN_CHUNK, DN_CHUNK), lambda c: (0, c, 0, 0, 0)),
                   pl.BlockSpec((2, None, width, DN_CHUNK), lambda c: (0, c, 0, 0)),
                   pl.BlockSpec((2, None, n_heads, 1, LANE), lambda c: (0, c, 0, 0, 0))],
        out_shape=[jax.ShapeDtypeStruct((2, r, width), F32),
                   jax.ShapeDtypeStruct((2, r, width), BF16),
                   jax.ShapeDtypeStruct((2, r, width), BF16),
                   jax.ShapeDtypeStruct((2, nc, n_heads, DN_CHUNK, DN_CHUNK), BF16),
                   jax.ShapeDtypeStruct((2, nc, width, DN_CHUNK), BF16),
                   jax.ShapeDtypeStruct((2, nc, n_heads, 1, LANE), F32)],
        compiler_params=_cparams(("parallel",), 40),
        name="dn_local",
    )(qkv, qkv, qkv, kt, bg, grow)


def _dn_seq_kernel(*refs, nh):
    ins, (of_ref, ob_ref, s_ref) = refs[:12], refs[12:]
    cs = DN_CHUNK
    hd = DN_HEAD_DIM

    @pl.when(pl.program_id(1) == 0)
    def _():
        s_ref[...] = jnp.zeros_like(s_ref)

    heads = [slice(h * hd, (h + 1) * hd) for h in range(nh)]
    for d, o_ref in enumerate((of_ref, ob_ref)):
        u_ref, w_ref, qg_ref, qk_ref, kdt_ref, eg_ref = ins[6 * d:6 * d + 6]
        lhs = jnp.stack([jnp.concatenate([w_ref[:, hs], qg_ref[:, hs]], axis=0) for hs in heads])
        s_old = s_ref[d]
        ws = _bdot(lhs, s_old.astype(BF16))
        u = jnp.stack([u_ref[:, hs] for hs in heads])
        v_new = (u - ws[:, :cs]).astype(BF16)
        lhs2 = jnp.concatenate([qk_ref[...], jnp.stack([kdt_ref[hs, :] for hs in heads])], axis=1)
        r2 = _bdot(lhs2, v_new)
        o = ws[:, cs:] + r2[:, :cs]
        for h, hs in enumerate(heads):
            o_ref[:, hs] = o[h]
        s_ref[d] = s_old * eg_ref[...] + r2[:, cs:]


def _dn_seq(u, w, qg, qk, kdt, eg, n_batch, ctx_len, seq, n_heads):
    r = u.shape[1]
    ncx = ctx_len // DN_CHUNK
    nlt = seq // DN_CHUNK
    width = n_heads * DN_HEAD_DIM

    def chunk(d):
        def f(b, s):
            pos_ctx = s if d == 0 else ncx - 1 - s
            pos_lat = s - ncx if d == 0 else nlt - 1 - (s - ncx)
            return jnp.where(s < ncx, b * ncx + pos_ctx, n_batch * ncx + b * nlt + pos_lat)
        return f

    in_specs, args = [], []
    for d in range(2):
        ch = chunk(d)
        in_specs += [
            pl.BlockSpec((None, DN_CHUNK, width), lambda b, s, ch=ch, d=d: (d, ch(b, s), 0)),
            pl.BlockSpec((None, DN_CHUNK, width), lambda b, s, ch=ch, d=d: (d, ch(b, s), 0)),
            pl.BlockSpec((None, DN_CHUNK, width), lambda b, s, ch=ch, d=d: (d, ch(b, s), 0)),
            pl.BlockSpec((None, None, n_heads, DN_CHUNK, DN_CHUNK),
                         lambda b, s, ch=ch, d=d: (d, ch(b, s), 0, 0, 0)),
            pl.BlockSpec((None, None, width, DN_CHUNK), lambda b, s, ch=ch, d=d: (d, ch(b, s), 0, 0)),
            pl.BlockSpec((None, None, n_heads, 1, LANE), lambda b, s, ch=ch, d=d: (d, ch(b, s), 0, 0, 0)),
        ]
        args += [u, w, qg, qk, kdt, eg]
    kern = functools.partial(_dn_seq_kernel, nh=n_heads)
    return pl.pallas_call(
        kern,
        grid=(n_batch, ncx + nlt),
        in_specs=in_specs,
        out_specs=[pl.BlockSpec((DN_CHUNK, width), lambda b, s, ch=chunk(0): (ch(b, s), 0)),
                   pl.BlockSpec((DN_CHUNK, width), lambda b, s, ch=chunk(1): (ch(b, s), 0))],
        out_shape=[jax.ShapeDtypeStruct((r, width), F32), jax.ShapeDtypeStruct((r, width), F32)],
        scratch_shapes=[pltpu.VMEM((2, n_heads, DN_HEAD_DIM, DN_HEAD_DIM), F32)],
        compiler_params=_cparams(("parallel", "arbitrary"), 40),
        name="dn_seq",
    )(*args)


def _sg_kernel(pu_ref, pv_ref, lnw_ref, lnb_ref, sgw_ref, sgbt_ref, y_ref, *, n_groups):
    tr = pu_ref.shape[0]
    gd = pu_ref.shape[1] // n_groups
    v = jax.nn.gelu(pv_ref[...].astype(F32))
    mu = jnp.mean(v, axis=-1, keepdims=True)
    xc = v - mu
    var = jnp.mean(xc * xc, axis=-1, keepdims=True)
    vv = (xc * lax.rsqrt(var + EPS) * lnw_ref[...] + lnb_ref[...]).astype(BF16)
    sgbt = sgbt_ref[...]
    for ch in range(tr // SG_CHUNK):
        rs = slice(ch * SG_CHUNK, (ch + 1) * SG_CHUNK)
        for g in range(n_groups):
            gs = slice(g * gd, (g + 1) * gd)
            mixed = _dot(sgw_ref[g], vv[rs, gs]) + sgbt[:, g:g + 1]
            u = jax.nn.gelu(pu_ref[rs, gs].astype(F32))
            y_ref[rs, gs] = (u * mixed).astype(y_ref.dtype)


def _sg(p_main, lnw, lnb, sgw, sgbt, layer, tr, u_blk, v_blk):
    r = p_main.shape[0]
    n_groups = sgw.shape[1]
    width = lnw.shape[1]
    kern = functools.partial(_sg_kernel, n_groups=n_groups)
    return pl.pallas_call(
        kern,
        grid=(r // tr,),
        in_specs=[pl.BlockSpec((tr, width), lambda i: (i, u_blk)),
                  pl.BlockSpec((tr, width), lambda i: (i, v_blk)),
                  pl.BlockSpec((1, width), lambda i: (0, 0)),
                  pl.BlockSpec((1, width), lambda i: (0, 0)),
                  pl.BlockSpec((None,) + sgw.shape[1:], lambda i: (layer, 0, 0, 0)),
                  pl.BlockSpec(sgbt.shape, lambda i: (0, 0))],
        out_specs=pl.BlockSpec((tr, width), lambda i: (i, 0)),
        out_shape=jax.ShapeDtypeStruct((r, width), BF16),
        compiler_params=_cparams(("parallel",), 32),
        name="spatial_gating",
    )(p_main, p_main, lnw, lnb, sgw, sgbt)


def _merge_kernel(of_ref, ob_ref, z_ref, dnw_ref, ysg_ref, wdn_ref, wsg_ref, gdn_ref, gsg_ref,
                  m_ref, ydn_ref):
    @pl.when(pl.program_id(1) == 0)
    def _():
        nw = dnw_ref[...]
        for h in range(of_ref.shape[1] // DN_HEAD_DIM):
            hs = slice(h * DN_HEAD_DIM, (h + 1) * DN_HEAD_DIM)
            o = of_ref[:, hs] + ob_ref[:, hs]
            ms = jnp.mean(o * o, axis=-1, keepdims=True)
            z = z_ref[:, hs].astype(F32)
            ydn_ref[:, hs] = (o * lax.rsqrt(ms + EPS) * nw * (z * jax.nn.sigmoid(z))).astype(BF16)

    a = _dot(ydn_ref[...], wdn_ref[...])
    b = _dot(ysg_ref[...], wsg_ref[...])
    m = (jax.nn.sigmoid(gdn_ref[...].astype(F32)) * a + jax.nn.sigmoid(gsg_ref[...].astype(F32)) * b)
    m_ref[...] = m.astype(m_ref.dtype)


def _merge(o_f, o_b, p_main, dnw, ysg, wdn, wsg, layer, tm, z_blk, gate_col0):
    r, width = ysg.shape
    d = wdn.shape[2]
    tn = min(512, d)
    g0 = gate_col0 // tn
    return pl.pallas_call(
        _merge_kernel,
        grid=(r // tm, d // tn),
        in_specs=[pl.BlockSpec((tm, width), lambda i, j: (i, 0)),
                  pl.BlockSpec((tm, width), lambda i, j: (i, 0)),
                  pl.BlockSpec((tm, width), lambda i, j: (i, z_blk)),
                  pl.BlockSpec((1, DN_HEAD_DIM), lambda i, j: (0, 0)),
                  pl.BlockSpec((tm, width), lambda i, j: (i, 0)),
                  pl.BlockSpec((None, width, tn), lambda i, j: (layer, 0, j)),
                  pl.BlockSpec((None, width, tn), lambda i, j: (layer, 0, j)),
                  pl.BlockSpec((tm, tn), lambda i, j: (i, g0 + j)),
                  pl.BlockSpec((tm, tn), lambda i, j: (i, g0 + d // tn + j))],
        out_specs=pl.BlockSpec((tm, tn), lambda i, j: (i, j)),
        out_shape=jax.ShapeDtypeStruct((r, d), BF16),
        scratch_shapes=[pltpu.VMEM((tm, width), BF16)],
        compiler_params=_cparams(("parallel", "arbitrary"), 48),
        name="merge",
    )(o_f, o_b, p_main, dnw, ysg, wdn, wsg, p_main, p_main)


def _out_proj_kernel(m_ref, w_ref, x_ref, ga_ref, o_ref):
    o_ref[...] = x_ref[...] + ga_ref[...] * _dot(m_ref[...], w_ref[...])


def _out_proj(m, w, layer, x, mod, tm, n_ctx_rows, seq):
    r, d = x.shape
    tn = min(1024, d)
    return pl.pallas_call(
        _out_proj_kernel,
        grid=(r // tm, d // tn),
        in_specs=[pl.BlockSpec((tm, d), lambda i, j: (i, 0)),
                  pl.BlockSpec((None, d, tn), lambda i, j: (layer, 0, j)),
                  pl.BlockSpec((tm, tn), lambda i, j: (i, j)),
                  _mod_spec(2, tm, n_ctx_rows, seq, d, ncol=tn)],
        out_specs=pl.BlockSpec((tm, tn), lambda i, j: (i, j)),
        out_shape=jax.ShapeDtypeStruct((r, d), F32),
        input_output_aliases={2: 0},
        compiler_params=_cparams(("parallel", "arbitrary"), 48),
        name="out_proj",
    )(m, w, x, mod)


def _pack_halves(xb):
    n2 = xb.shape[1] // 2
    lo = lax.bitcast_convert_type(xb[:, :n2].astype(F32), jnp.uint32)
    hi = lax.bitcast_convert_type(xb[:, n2:].astype(F32), jnp.uint32)
    return (lo >> 16) | (hi & jnp.uint32(0xFFFF0000))


def _unpack_halves_f32(w):
    lo = lax.bitcast_convert_type(w << 16, F32)
    hi = lax.bitcast_convert_type(w & jnp.uint32(0xFFFF0000), F32)
    return jnp.concatenate([lo, hi], axis=1)


def _unpack_halves(w):
    return _unpack_halves_f32(w).astype(BF16)


def _nt_dot(a, b):
    return lax.dot_general(a, b, (((1,), (1,)), ((), ())), preferred_element_type=F32)


def _route_kernel(x_ref, sh_ref, sc_ref, nw_ref, wrt_ref, bias_ref, h_ref, hp_ref, g_ref, eid_ref, wk_ref):
    x = x_ref[...]
    ms = jnp.mean(x * x, axis=-1, keepdims=True)
    h = x * lax.rsqrt(ms + EPS) * nw_ref[...] * (1.0 + sc_ref[...]) + sh_ref[...]
    hb = h.astype(BF16)
    h_ref[...] = hb
    hp_ref[...] = _pack_halves(hb)
    logits = _nt_dot(wrt_ref[...].astype(BF16), hb)
    s = jax.nn.sigmoid(logits)
    sel = s + bias_ref[...]
    ne, tm = sel.shape
    gsz = ne // N_EXPERT_GROUPS
    sub = lax.broadcasted_iota(jnp.int32, (gsz, tm), 0)
    gs_rows = []
    for g in range(N_EXPERT_GROUPS):
        blk = sel[g * gsz:(g + 1) * gsz, :]
        m1 = jnp.max(blk, axis=0, keepdims=True)
        i1 = jnp.min(jnp.where(blk == m1, sub, gsz), axis=0, keepdims=True)
        m2 = jnp.max(jnp.where(sub == i1, -jnp.inf, blk), axis=0, keepdims=True)
        gs_rows.append(m1 + m2)
    masked_blocks = []
    for g in range(N_EXPERT_GROUPS):
        rank = jnp.zeros((1, tm), F32)
        for g2 in range(N_EXPERT_GROUPS):
            if g2 == g:
                continue
            ahead = (gs_rows[g2] > gs_rows[g]) if g2 > g else (gs_rows[g2] >= gs_rows[g])
            rank = rank + jnp.where(ahead, 1.0, 0.0)
        keep = rank < TOPK_GROUPS
        masked_blocks.append(jnp.where(keep, sel[g * gsz:(g + 1) * gsz, :], -jnp.inf))
    masked = jnp.concatenate(masked_blocks, axis=0)
    eidx = lax.broadcasted_iota(jnp.int32, (ne, tm), 0)
    rank = jnp.zeros((ne, tm), F32)
    for e2 in range(ne):
        r2 = masked[e2:e2 + 1, :]
        tie = jnp.where(eidx > e2, 1.0, 0.0)
        rank = rank + jnp.where(r2 > masked, 1.0, jnp.where(r2 == masked, tie, 0.0))
    wts = jnp.where(rank < TOP_K, s, 0.0)
    gates = wts / jnp.sum(wts, axis=0, keepdims=True) * ROUTED_SCALE
    g_ref[...] = gates
    ids, wks = [], []
    for k in range(eid_ref.shape[0]):
        hit = rank == k
        ids.append(jnp.sum(jnp.where(hit, eidx, 0), axis=0, keepdims=True))
        wks.append(jnp.sum(jnp.where(hit, gates, 0.0), axis=0, keepdims=True))
    eid_ref[...] = jnp.concatenate(ids, axis=0)
    wk_ref[...] = jnp.concatenate(wks, axis=0)


def _route(x, mod, nw, wrt, bias, tm, n_ctx_rows, seq):
    r, d = x.shape
    ne = wrt.shape[0]
    return pl.pallas_call(
        _route_kernel,
        grid=(r // tm,),
        in_specs=[pl.BlockSpec((tm, d), lambda i: (i, 0)),
                  _mod_spec(3, tm, n_ctx_rows, seq, d),
                  _mod_spec(4, tm, n_ctx_rows, seq, d),
                  pl.BlockSpec((1, d), lambda i: (0, 0)),
                  pl.BlockSpec((ne, d), lambda i: (0, 0)),
                  pl.BlockSpec((ne, 1), lambda i: (0, 0))],
        out_specs=[pl.BlockSpec((tm, d), lambda i: (i, 0)),
                   pl.BlockSpec((tm, d // 2), lambda i: (i, 0)),
                   pl.BlockSpec((ne, tm), lambda i: (0, i)),
                   pl.BlockSpec((8, tm), lambda i: (0, i)),
                   pl.BlockSpec((8, tm), lambda i: (0, i))],
        out_shape=[jax.ShapeDtypeStruct((r, d), BF16), jax.ShapeDtypeStruct((r, d // 2), jnp.uint32),
                   jax.ShapeDtypeStruct((ne, r), F32), jax.ShapeDtypeStruct((8, r), jnp.int32),
                   jax.ShapeDtypeStruct((8, r), F32)],
        compiler_params=_cparams(("parallel",), 48),
        name="route",
    )(x, mod, mod, nw, wrt, bias)


def _shared_ffn_kernel(h_ref, wg_ref, wu_ref, wd_ref, y_ref):
    @pl.when(pl.program_id(1) == 0)
    def _():
        y_ref[...] = jnp.zeros_like(y_ref)

    h = h_ref[...]
    a = _dot(h, wg_ref[...])
    u = _dot(h, wu_ref[...])
    y_ref[...] += _dot((a * jax.nn.sigmoid(a) * u).astype(BF16), wd_ref[...])


def _shared_ffn(h, wg, wu, wd, layer, ff, tm):
    r, d = h.shape
    return pl.pallas_call(
        _shared_ffn_kernel,
        grid=(r // tm, wg.shape[2] // ff),
        in_specs=[pl.BlockSpec((tm, d), lambda i, e: (i, 0)),
                  pl.BlockSpec((None, d, ff), lambda i, e: (layer, 0, e)),
                  pl.BlockSpec((None, d, ff), lambda i, e: (layer, 0, e)),
                  pl.BlockSpec((None, ff, d), lambda i, e: (layer, e, 0))],
        out_specs=pl.BlockSpec((tm, d), lambda i, e: (i, 0)),
        out_shape=jax.ShapeDtypeStruct((r, d), F32),
        compiler_params=_cparams(("parallel", "arbitrary"), 48),
        name="shared_ffn",
    )(h, wg, wu, wd)


def _slot_wait(buf, sem, slot):
    pltpu.make_async_copy(buf.at[slot], buf.at[slot], sem.at[slot]).wait()


def _moe_group_kernel(tok_ref, texp_ref, h_hbm, wg_ref, wu_ref, wd_ref, gate_ref, y_ref, buf, sem,
                      *, tmx, nt):
    del texp_ref
    i = pl.program_id(0)

    def row_copy(tile, slot, r):
        t = tok_ref[tile * tmx + r]
        return pltpu.make_async_copy(h_hbm.at[pl.ds(t, 1)], buf.at[slot, pl.ds(r, 1)], sem.at[slot])

    @pl.when(i == 0)
    def _():
        def body(r, carry):
            row_copy(0, 0, r).start()
            return carry
        lax.fori_loop(0, tmx, body, 0)

    slot = i % 2
    _slot_wait(buf, sem, slot)
    nxt = jnp.minimum(i + 1, nt - 1)
    n_row_tiles = tmx // SUBLANES
    for j in range(tmx):
        r = (j % n_row_tiles) * SUBLANES + j // n_row_tiles
        row_copy(nxt, 1 - slot, r).start(priority=j % DMA_THREADS)
    x = _unpack_halves(buf[slot])
    a = _dot(x, wg_ref[...])
    u = _dot(x, wu_ref[...])
    hid = (a * jax.nn.sigmoid(a) * u * gate_ref[...]).astype(BF16)
    y_ref[...] = _pack_halves(_dot(hid, wd_ref[...]).astype(BF16))

    @pl.when(i == nt - 1)
    def _():
        _slot_wait(buf, sem, 1 - slot)


def _moe_group(tok_pad, tile_exp, hp, wg, wu, wd, layer, gate_pad, tmx):
    p = tok_pad.shape[0]
    nt = p // tmx
    d = 2 * hp.shape[1]
    ff = wg.shape[3]
    kern = functools.partial(_moe_group_kernel, tmx=tmx, nt=nt)
    grid_spec = pltpu.PrefetchScalarGridSpec(
        num_scalar_prefetch=2,
        grid=(nt,),
        in_specs=[pl.BlockSpec(memory_space=pl.ANY),
                  pl.BlockSpec((None, None, d, ff), lambda i, tok, te: (layer, te[i], 0, 0)),
                  pl.BlockSpec((None, None, d, ff), lambda i, tok, te: (layer, te[i], 0, 0)),
                  pl.BlockSpec((None, None, ff, d), lambda i, tok, te: (layer, te[i], 0, 0)),
                  pl.BlockSpec((tmx, 1), lambda i, tok, te: (i, 0))],
        out_specs=pl.BlockSpec((tmx, d // 2), lambda i, tok, te: (i, 0)),
        scratch_shapes=[pltpu.VMEM((2, tmx, d // 2), jnp.uint32), pltpu.SemaphoreType.DMA((2,))],
    )
    return pl.pallas_call(
        kern,
        grid_spec=grid_spec,
        out_shape=jax.ShapeDtypeStruct((p, d // 2), jnp.uint32),
        compiler_params=_cparams(("arbitrary",), 48),
        name="moe_group",
    )(tok_pad, tile_exp, hp, wg, wu, wd, gate_pad)


def _moe_combine_kernel(pos_ref, y_hbm, x_ref, ys_ref, ga_ref, *rest, tt, nk, nt, t0, final):
    if final:
        fw_ref, o_ref, buf, sem = rest
    else:
        o_ref, buf, sem = rest
    i = pl.program_id(0)

    def row_copy(tile, slot, r, k):
        p = pos_ref[((tile + t0) * tt + r) * nk + k]
        return pltpu.make_async_copy(y_hbm.at[pl.ds(p, 1)], buf.at[slot, k, pl.ds(r, 1)], sem.at[slot])

    @pl.when(i == 0)
    def _():
        def body(r, carry):
            for k in range(nk):
                row_copy(0, 0, r, k).start()
            return carry
        lax.fori_loop(0, tt, body, 0)

    slot = i % 2
    _slot_wait(buf, sem, slot)
    nxt = jnp.minimum(i + 1, nt - 1)
    for r in range(tt):
        for k in range(nk):
            row_copy(nxt, 1 - slot, r, k).start(priority=k % DMA_THREADS)
    routed = _unpack_halves_f32(buf[slot, 0])
    for k in range(1, nk):
        routed = routed + _unpack_halves_f32(buf[slot, k])
    x = x_ref[...] + ga_ref[...] * (ys_ref[...] + routed)
    if final:
        ms = jnp.mean(x * x, axis=-1, keepdims=True)
        x = x * lax.rsqrt(ms + EPS) * fw_ref[...]
    o_ref[...] = x

    @pl.when(i == nt - 1)
    def _():
        _slot_wait(buf, sem, 1 - slot)


def _moe_combine(pos_flat, y_sorted, x, ys, mod, tt, n_ctx_rows, seq, final_w=None):
    r, d = x.shape
    nk = pos_flat.shape[0] // r
    final = final_w is not None
    t0 = n_ctx_rows // tt if final else 0
    nt = r // tt - t0
    row_spec = pl.BlockSpec((tt, d), lambda i, pos: (i + t0, 0))
    in_specs = [pl.BlockSpec(memory_space=pl.ANY), row_spec, row_spec,
                _mod_spec(5, tt, n_ctx_rows, seq, d, tile0=t0)]
    args = [pos_flat, y_sorted, x, ys, mod]
    if final:
        in_specs.append(pl.BlockSpec((1, d), lambda i, pos: (0, 0)))
        args.append(final_w)
    kern = functools.partial(_moe_combine_kernel, tt=tt, nk=nk, nt=nt, t0=t0, final=final)
    grid_spec = pltpu.PrefetchScalarGridSpec(
        num_scalar_prefetch=1,
        grid=(nt,),
        in_specs=in_specs,
        out_specs=pl.BlockSpec((tt, d), lambda i, pos: (i, 0)),
        scratch_shapes=[pltpu.VMEM((2, nk, tt, d // 2), jnp.uint32), pltpu.SemaphoreType.DMA((2,))],
    )
    return pl.pallas_call(
        kern,
        grid_spec=grid_spec,
        out_shape=jax.ShapeDtypeStruct((nt * tt, d), F32),
        compiler_params=_cparams(("arbitrary",), 48),
        name="moe_combine",
    )(*args)


def _moe_plan(gates_t, eid_t, wk_t, tmx, p_rows):
    ne, r = gates_t.shape
    eid = eid_t[:TOP_K].T
    wk = wk_t[:TOP_K].T
    sel = (gates_t > 0).astype(jnp.int32)
    pos_in_e = jnp.cumsum(sel, axis=1) - 1
    cnt = pos_in_e[:, -1] + 1
    padded = ((cnt + tmx - 1) // tmx) * tmx
    ends = jnp.cumsum(padded)
    gstart = ends - padded
    cstart = jnp.cumsum(cnt) - cnt
    pos = gstart[eid] + jnp.take_along_axis(pos_in_e.T, eid, axis=1)
    pos_flat = pos.reshape(-1)
    tok_flat = jnp.repeat(jnp.arange(r, dtype=jnp.int32), TOP_K)
    _, tok_s, gate_s = lax.sort((pos_flat, tok_flat, wk.reshape(-1)), num_keys=1)
    nt = p_rows // tmx
    tile_exp = jnp.minimum(jnp.sum(ends[None, :] <= (jnp.arange(nt) * tmx)[:, None], axis=1), ne - 1)
    tile_exp = tile_exp.astype(jnp.int32)
    j = (jnp.arange(nt, dtype=jnp.int32) * tmx - gstart[tile_exp])[:, None] + jnp.arange(tmx, dtype=jnp.int32)[None, :]
    valid = j < cnt[tile_exp][:, None]
    ci = jnp.clip(cstart[tile_exp][:, None] + j, 0, r * TOP_K - 1).reshape(-1)
    valid = valid.reshape(-1)
    tok_pad = jnp.where(valid, tok_s[ci], 0).astype(jnp.int32)
    gate_pad = jnp.where(valid, gate_s[ci], 0.0)[:, None]
    return tok_pad, gate_pad, tile_exp, pos_flat.astype(jnp.int32)


def kernel(x, c, ctx, c_ctx, w_ada, b_ada, norm1_w, w_in, dn_conv_w, dn_a_log, dn_dt_bias, dn_norm_w, sg_ln_w, sg_ln_b, sg_w, sg_b, w_br_dn, w_br_sg, w_out, norm2_w, w_router, router_bias, w_exp_gate, w_exp_up, w_exp_down, w_sh_gate, w_sh_up, w_sh_down, final_norm_w):
    nb, seq, d = x.shape
    ctx_len = ctx.shape[1]
    depth = w_ada.shape[0]
    n_heads = dn_a_log.shape[-1]
    dn_width = dn_conv_w.shape[-1] // 3
    sg_width = sg_ln_w.shape[-1]
    ne, _, ff = w_exp_gate.shape[1:]
    sh_ff = w_sh_gate.shape[-1]
    n_ctx_rows = nb * ctx_len
    r = n_ctx_rows + nb * seq
    assert nb + 1 <= 8 and dn_width == n_heads * DN_HEAD_DIM and sh_ff % ff == 0
    tm = min(512, n_ctx_rows)
    tc = min(256, ctx_len)
    assert n_ctx_rows % tm == 0 and seq % tm == 0 and ctx_len % tc == 0 and seq % tc == 0
    tmx = 256
    tt = 64
    p_rows = -(-(r * TOP_K) // tmx) * tmx + ne * tmx
    segs =[(b * ctx_len, ctx_len) for b in range(nb)] + [(n_ctx_rows + b * seq, seq) for b in range(nb)]

    n_ba = 4 * n_heads
    c_qkv = 3 * dn_width
    z_col = c_qkv
    u_col = z_col + dn_width
    v_col = u_col + sg_width
    gate_col = v_col + sg_width
    assert dn_width == sg_width and z_col % dn_width == 0 and gate_col % 512 == 0

    cvec = jnp.concatenate([c_ctx[None, :], c, jnp.zeros((8 - 1 - nb, d), F32)], axis=0)
    mod_all = _ada(cvec, w_ada, b_ada)
    xa = jnp.concatenate([ctx.reshape(n_ctx_rows, d), x.reshape(nb * seq, d)], axis=0)

    w_main = jnp.concatenate([w_in[:, :, :c_qkv], w_in[:, :, c_qkv + n_ba:]], axis=2).astype(BF16)
    w_ba = jnp.pad(w_in[:, :, c_qkv:c_qkv + n_ba], ((0, 0), (0, 0), (0, LANE - n_ba))).astype(BF16)
    sgw_b, wdn_b, wsg_b, wout_b = (a.astype(BF16) for a in (sg_w, w_br_dn, w_br_sg, w_out))
    weg_b, weu_b, wed_b = (a.astype(BF16) for a in (w_exp_gate, w_exp_up, w_exp_down))
    wsg_sh_b, wsu_sh_b, wsd_sh_b = (a.astype(BF16) for a in (w_sh_gate, w_sh_up, w_sh_down))
    pad_ba = ((0, 0), (2 * n_heads, LANE - 4 * n_heads))

    out = None
    for l in range(depth):
        last = l == depth - 1
        mod = mod_all[l, :nb + 1].reshape(nb + 1, N_MOD, 1, d)

        p_main, pba = _in_proj(xa, mod, norm1_w[l][None, :], w_main, w_ba, l, tm, n_ctx_rows, seq)
        qkv, bg = _dn_prep(p_main, pba, dn_conv_w[l], jnp.pad(dn_a_log[l].reshape(1, -1), pad_ba),
                           jnp.pad(dn_dt_bias[l].reshape(1, -1), pad_ba), tc, segs, dn_width, n_heads)
        grow = bg[:, 2 * n_heads:4 * n_heads].reshape(r // DN_CHUNK, DN_CHUNK, 2 * n_heads).transpose(0, 2, 1)
        grow = jnp.pad(grow, ((0, 0), (0, max(0, BF16_SUBLANES - 2 * n_heads)), (0, 0)))
        kt = qkv[:, dn_width:2 * dn_width].reshape(r // DN_CHUNK, DN_CHUNK, dn_width).transpose(0, 2, 1)
        o_f, o_b = _dn_seq(*_dn_local(qkv, kt, bg, grow, n_heads), nb, ctx_len, seq, n_heads)
        ysg = _sg(p_main, sg_ln_w[l][None, :], sg_ln_b[l][None, :], sgw_b, sg_b[l].T, l,
                  min(256, ctx_len), u_col // sg_width, v_col // sg_width)
        m = _merge(o_f, o_b, p_main, dn_norm_w[l][None, :], ysg, wdn_b, wsg_b, l, tm, z_col // dn_width, gate_col)
        xa = _out_proj(m, wout_b, l, xa, mod, tm, n_ctx_rows, seq)

        h2, h2p, gates_t, eid_t, wk_t = _route(xa, mod, norm2_w[l][None, :], w_router[l].T,
                                               router_bias[l][:, None], min(256, tm), n_ctx_rows, seq)
        tok_pad, gate_pad, tile_exp, pos_flat = _moe_plan(gates_t, eid_t, wk_t, tmx, p_rows)
        y_sorted = _moe_group(tok_pad, tile_exp, h2p, weg_b, weu_b, wed_b, l, gate_pad, tmx)
        y_shared = _shared_ffn(h2, wsg_sh_b, wsu_sh_b, wsd_sh_b, l, ff, tm)
        if last:
            out = _moe_combine(pos_flat, y_sorted, xa, y_shared, mod, tt, n_ctx_rows, seq, final_norm_w[None, :])
        else:
            xa = _moe_combine(pos_flat, y_sorted, xa, y_shared, mod, tt, n_ctx_rows, seq)
    return out.reshape(nb, seq, d)
```
